```python
import jax, jax.numpy as jnp
from jax import lax
import numpy as np

D_MODEL = 2048
BATCH = 8
SEQ = 2048
DEPTH = 1

MIX_WIDTH = D_MODEL
HEAD_DIM = 128
ATTN_WIDTH = MIX_WIDTH // 2
ATTN_HEADS = ATTN_WIDTH // HEAD_DIM
GMLP_GROUP_DIM = 128
GMLP_WIDTH = MIX_WIDTH - ATTN_WIDTH
GMLP_GROUPS = GMLP_WIDTH // GMLP_GROUP_DIM
CHUNK = 128
Q_BLOCK = 128
D_FF = 4 * D_MODEL
N_MOD = 6
IN_COLS = 3 * ATTN_WIDTH + ATTN_HEADS + 2 * GMLP_WIDTH
EPS = 1e-6

kernel_name = "hybrid_fox_gmlp_adaln_block"


def rms_norm(x, g):
    xf = x.astype(jnp.float32)
    y = xf * lax.rsqrt(jnp.mean(xf * xf, axis=-1, keepdims=True) + EPS)
    return (y * g.astype(jnp.float32)).astype(x.dtype)


def layer_norm_nobias(x, g):
    xf = x.astype(jnp.float32)
    mu = jnp.mean(xf, axis=-1, keepdims=True)
    xc = xf - mu
    y = xc * lax.rsqrt(jnp.mean(xc * xc, axis=-1, keepdims=True) + EPS)
    return y * g.astype(jnp.float32)


def modulate(h, shift, scale):
    return h * (1.0 + scale[:, None, :]) + shift[:, None, :]


def forgetting_attention(q, k, v, f_logit):
    B, S, H, Dh = q.shape
    nblk = S // Q_BLOCK
    scale = 1.0 / np.sqrt(Dh).astype(np.float32)
    F = jnp.cumsum(jax.nn.log_sigmoid(f_logit.astype(jnp.float32)), axis=1)
    F_bhs = F.transpose(0, 2, 1)
    kf = k.astype(jnp.float32)
    vf = v.astype(jnp.float32)
    qb = q.astype(jnp.float32).reshape(B, nblk, Q_BLOCK, H, Dh).transpose(1, 0, 3, 2, 4)
    Fq = F_bhs.reshape(B, H, nblk, Q_BLOCK).transpose(2, 0, 1, 3)
    kpos = jnp.arange(S)

    def block(args):
        i, q_blk, F_blk = args
        s = jnp.einsum('bhqd,bshd->bhqs', q_blk, kf) * scale
        s = s + F_blk[..., None] - F_bhs[:, :, None, :]
        qpos = i * Q_BLOCK + jnp.arange(Q_BLOCK)
        causal = kpos[None, :] <= qpos[:, None]
        s = jnp.where(causal[None, None], s, -jnp.inf)
        p = jax.nn.softmax(s, axis=-1)
        return jnp.einsum('bhqs,bshd->bqhd', p, vf)

    out = lax.map(block, (jnp.arange(nblk), qb, Fq))
    return out.transpose(1, 0, 2, 3, 4).reshape(B, S, H * Dh)


def chunked_spatial_gating(u, v, g_ln, w_s, b_s):
    B, S, G, C = v.shape
    n = S // CHUNK
    vn = layer_norm_nobias(v, g_ln)
    vc = vn.reshape(B, n, CHUNK, G, C)
    mask = jnp.tril(jnp.ones((CHUNK, CHUNK), dtype=jnp.float32))
    w = w_s.astype(jnp.float32) * mask[None]
    z = jnp.einsum('gts,bnsgc->bntgc', w, vc) + b_s.astype(jnp.float32).T[None, None, :, :, None]
    out = u.astype(jnp.float32) * z.reshape(B, S, G, C)
    return out.reshape(B, S, G * C)


def setup_inputs(seed: int = 0) -> dict:
    key = jax.random.key(seed)
    ks = jax.random.split(key, 20)
    f32 = jnp.float32
    nrm = lambda k, shp, s: jax.random.normal(k, shp, f32) * s
    gain = lambda k, shp: 1.0 + 0.02 * jax.random.normal(k, shp, f32)
    x = jax.random.normal(ks[0], (BATCH, SEQ, D_MODEL), f32)
    c = jax.random.normal(ks[1], (BATCH, D_MODEL), f32)
    w_ada = nrm(ks[2], (DEPTH, D_MODEL, N_MOD * D_MODEL), 0.5 * D_MODEL ** -0.5)
    b_ada = nrm(ks[3], (DEPTH, N_MOD * D_MODEL), 0.02)
    g_mix = gain(ks[4], (DEPTH, D_MODEL))
    w_in = nrm(ks[5], (DEPTH, D_MODEL, IN_COLS), D_MODEL ** -0.5)
    b_f = jax.random.uniform(ks[6], (DEPTH, ATTN_HEADS), f32, 1.0, 4.0)
    g_q = gain(ks[7], (DEPTH, HEAD_DIM))
    g_k = gain(ks[8], (DEPTH, HEAD_DIM))
    g_sgu = gain(ks[9], (DEPTH, GMLP_GROUPS, GMLP_GROUP_DIM))
    w_s = nrm(ks[10], (DEPTH, GMLP_GROUPS, CHUNK, CHUNK), 0.5 * CHUNK ** -0.5)
    b_s = 1.0 + nrm(ks[11], (DEPTH, GMLP_GROUPS, CHUNK), 0.1)
    w_out = nrm(ks[12], (DEPTH, MIX_WIDTH, D_MODEL), MIX_WIDTH ** -0.5)
    g_ffn = gain(ks[13], (DEPTH, D_MODEL))
    w_ff1 = nrm(ks[14], (DEPTH, D_MODEL, D_FF), D_MODEL ** -0.5)
    w_ff2 = nrm(ks[15], (DEPTH, D_FF, D_MODEL), D_FF ** -0.5)
    return {"x": x, "c": c, "w_ada": w_ada, "b_ada": b_ada, "g_mix": g_mix, "w_in": w_in,
            "b_f": b_f, "g_q": g_q, "g_k": g_k, "g_sgu": g_sgu, "w_s": w_s, "b_s": b_s,
            "w_out": w_out, "g_ffn": g_ffn, "w_ff1": w_ff1, "w_ff2": w_ff2}


def reference(x, c, w_ada, b_ada, g_mix, w_in, b_f, g_q, g_k, g_sgu, w_s, b_s,
              w_out, g_ffn, w_ff1, w_ff2):
    B, S, D = x.shape
    H, Dh, G, C = ATTN_HEADS, HEAD_DIM, GMLP_GROUPS, GMLP_GROUP_DIM
    c_act = jax.nn.silu(c)
    for l in range(DEPTH):
        mod = c_act @ w_ada[l] + b_ada[l]
        shift1, scale1, gate1, shift2, scale2, gate2 = jnp.split(mod, N_MOD, axis=-1)

        h = modulate(rms_norm(x, g_mix[l]), shift1, scale1)
        proj = h @ w_in[l]
        o = 0
        q = proj[..., o:o + ATTN_WIDTH]; o += ATTN_WIDTH
        k = proj[..., o:o + ATTN_WIDTH]; o += ATTN_WIDTH
        v = proj[..., o:o + ATTN_WIDTH]; o += ATTN_WIDTH
        f_logit = proj[..., o:o + H]; o += H
        u_g = proj[..., o:o + GMLP_WIDTH]; o += GMLP_WIDTH
        v_g = proj[..., o:o + GMLP_WIDTH]

        q = rms_norm(q.reshape(B, S, H, Dh), g_q[l])
        k = rms_norm(k.reshape(B, S, H, Dh), g_k[l])
        v = v.reshape(B, S, H, Dh)
        attn = forgetting_attention(q, k, v, f_logit + b_f[l]).astype(x.dtype)

        u_g = jax.nn.gelu(u_g).reshape(B, S, G, C)
        v_g = jax.nn.gelu(v_g).reshape(B, S, G, C)
        sgu = chunked_spatial_gating(u_g, v_g, g_sgu[l], w_s[l], b_s[l]).astype(x.dtype)

        mix = jnp.concatenate([attn, sgu], axis=-1) @ w_out[l]
        x = x + gate1[:, None, :] * mix

        h2 = modulate(rms_norm(x, g_ffn[l]), shift2, scale2)
        ff = jnp.square(jax.nn.relu(h2 @ w_ff1[l])) @ w_ff2[l]
        x = x + gate2[:, None, :] * ff
    return x
```

```python
import functools

import jax
import jax.numpy as jnp
import numpy as np
from jax import lax
from jax.experimental import pallas as pl
from jax.experimental.pallas import tpu as pltpu

_F32 = jnp.float32
_BF16 = jnp.bfloat16

HEAD_DIM = 128
CHUNK = 128
N_MOD = 6
EPS = 1e-6
LANES = 128

_SHIFT1, _SCALE1, _GATE1, _SHIFT2, _SCALE2, _GATE2 = range(6)

_MIB = 1024 * 1024


def _params(semantics, vmem_mib):
    return pltpu.CompilerParams(dimension_semantics=semantics,
                                vmem_limit_bytes=vmem_mib * _MIB)


def _adaln_kernel(c_ref, w_ref, b_ref, o_ref):
    c = c_ref[...]
    c_act = (c * jax.nn.sigmoid(c)).astype(_BF16)
    w = w_ref[...].astype(_BF16)
    o_ref[...] = jnp.dot(c_act, w, preferred_element_type=_F32) + b_ref[...]


def _adaln(c, w_ada, b_ada, *, tn=1024):
    B, D = c.shape
    N = w_ada.shape[1]
    return pl.pallas_call(
        _adaln_kernel,
        grid=(N // tn,),
        in_specs=[pl.BlockSpec((B, D), lambda n: (0, 0)),
                  pl.BlockSpec((D, tn), lambda n: (0, n)),
                  pl.BlockSpec((1, tn), lambda n: (0, n))],
        out_specs=pl.BlockSpec((B, tn), lambda n: (0, n)),
        out_shape=jax.ShapeDtypeStruct((B, N), _F32),
        compiler_params=_params(("parallel",), 40),
        name="adaln",
    )(c, w_ada, b_ada.reshape(1, N))


def _norm_modulate(x, g, shift, scale):
    ms = jnp.mean(x * x, axis=-1, keepdims=True)
    y = x * lax.rsqrt(ms + EPS) * g
    return y * (1.0 + scale) + shift


def _inproj_kernel(x_ref, mod_ref, gmix_ref, w_ref, wf_ref, gq_ref, gk_ref, gsgu_ref,
                   q_ref, k_ref, v_ref, u_ref, vn_ref, f_ref, h_scr, *, q_scale):
    n = pl.program_id(1)
    width = w_ref.shape[-1]
    n_heads = width // HEAD_DIM

    @pl.when(n == 0)
    def _():
        h = _norm_modulate(x_ref[...], gmix_ref[...],
                           mod_ref[0, _SHIFT1:_SHIFT1 + 1, :], mod_ref[0, _SCALE1:_SCALE1 + 1, :])
        hb = h.astype(_BF16)
        h_scr[...] = hb
        f_ref[...] = jnp.dot(hb, wf_ref[...], preferred_element_type=_F32)

    acc = jnp.dot(h_scr[...], w_ref[...], preferred_element_type=_F32)

    def head_rms(o_ref, g, scale):
        for hh in range(n_heads):
            sl = slice(hh * HEAD_DIM, (hh + 1) * HEAD_DIM)
            blk = acc[:, sl]
            ms = jnp.mean(blk * blk, axis=-1, keepdims=True)
            y = blk * lax.rsqrt(ms + EPS) * g
            if scale is not None:
                y = y * scale
            o_ref[:, sl] = y.astype(o_ref.dtype)

    @pl.when(n == 0)
    def _():
        head_rms(q_ref, gq_ref[...], q_scale)

    @pl.when(n == 1)
    def _():
        head_rms(k_ref, gk_ref[...], None)

    @pl.when(n == 2)
    def _():
        v_ref[...] = acc.astype(v_ref.dtype)

    @pl.when(n == 3)
    def _():
        u_ref[...] = jax.nn.gelu(acc).astype(u_ref.dtype)

    @pl.when(n == 4)
    def _():
        for gg in range(n_heads):
            sl = slice(gg * HEAD_DIM, (gg + 1) * HEAD_DIM)
            blk = jax.nn.gelu(acc[:, sl])
            mu = jnp.mean(blk, axis=-1, keepdims=True)
            xc = blk - mu
            var = jnp.mean(xc * xc, axis=-1, keepdims=True)
            y = xc * lax.rsqrt(var + EPS) * gsgu_ref[:, sl]
            vn_ref[:, sl] = y.astype(vn_ref.dtype)


def _inproj(x2, mod3, g_mix, w5, w_f, g_q, g_k, g_sgu, *, seq, tm=512):
    M, D = x2.shape
    n_sec, _, width = w5.shape
    tiles_per_seq = seq // tm
    row = lambda m, n: (m, 0)
    const2 = lambda m, n: (0, 0)
    act = jax.ShapeDtypeStruct((M, width), _BF16)
    return pl.pallas_call(
        functools.partial(_inproj_kernel, q_scale=np.float32(1.0 / np.sqrt(HEAD_DIM))),
        grid=(M // tm, n_sec),
        in_specs=[pl.BlockSpec((tm, D), row),
                  pl.BlockSpec((1, N_MOD, D), lambda m, n: (m // tiles_per_seq, 0, 0)),
                  pl.BlockSpec((1, D), const2),
                  pl.BlockSpec((None, D, width), lambda m, n: (n, 0, 0)),
                  pl.BlockSpec((D, LANES), const2),
                  pl.BlockSpec((1, HEAD_DIM), const2),
                  pl.BlockSpec((1, HEAD_DIM), const2),
                  pl.BlockSpec((1, width), const2)],
        out_specs=[pl.BlockSpec((tm, width), row)] * 5 + [pl.BlockSpec((tm, LANES), row)],
        out_shape=[act] * 5 + [jax.ShapeDtypeStruct((M, LANES), _F32)],
        scratch_shapes=[pltpu.VMEM((tm, D), _BF16)],
        compiler_params=_params(("parallel", "arbitrary"), 48),
        name="inproj",
    )(x2, mod3, g_mix, w5, w_f, g_q, g_k, g_sgu)


def _fcumsum_kernel(f_ref, bf_ref, fcol_ref, frow_ref, *, n_heads):
    S = f_ref.shape[0]
    r = lax.broadcasted_iota(jnp.int32, (CHUNK, CHUNK), 0)
    cidx = lax.broadcasted_iota(jnp.int32, (CHUNK, CHUNK), 1)
    tri = jnp.where(cidx <= r, 1.0, 0.0).astype(_BF16)
    carry = jnp.zeros((1, LANES), _F32)
    for ci in range(S // CHUNK):
        sl = slice(ci * CHUNK, (ci + 1) * CHUNK)
        lf = jax.nn.log_sigmoid(f_ref[sl, :] + bf_ref[...])
        hi = lf.astype(_BF16)
        r1 = lf - hi.astype(_F32)
        mid = r1.astype(_BF16)
        lo = (r1 - mid.astype(_F32)).astype(_BF16)
        cs = (jnp.dot(tri, hi, preferred_element_type=_F32)
              + jnp.dot(tri, mid, preferred_element_type=_F32)
              + jnp.dot(tri, lo, preferred_element_type=_F32))
        F = cs + carry
        fcol_ref[sl, :] = F
        frow_ref[0, :, sl] = F.T[0:n_heads, :]
        carry = F[CHUNK - 1:CHUNK, :]


def _fcumsum(flog, b_f_pad, *, batch, seq, n_heads):
    return pl.pallas_call(
        functools.partial(_fcumsum_kernel, n_heads=n_heads),
        grid=(batch,),
        in_specs=[pl.BlockSpec((seq, LANES), lambda b: (b, 0)),
                  pl.BlockSpec((1, LANES), lambda b: (0, 0))],
        out_specs=[pl.BlockSpec((seq, LANES), lambda b: (b, 0)),
                   pl.BlockSpec((1, n_heads, seq), lambda b: (b, 0, 0))],
        out_shape=[jax.ShapeDtypeStruct((batch * seq, LANES), _F32),
                   jax.ShapeDtypeStruct((batch, n_heads, seq), _F32)],
        compiler_params=_params(("parallel",), 32),
        name="fcumsum",
    )(flog, b_f_pad)


def _attn_kernel(q_ref, k_ref, v_ref, fcol_ref, frow_ref, o_ref, *, tq):
    h = pl.program_id(1)
    n_blk = q_ref.shape[1] // tq
    lane = lax.broadcasted_iota(jnp.int32, (tq, LANES), 1)
    rows = lax.broadcasted_iota(jnp.int32, (tq, tq), 0)
    cols = lax.broadcasted_iota(jnp.int32, (tq, tq), 1)
    causal = cols <= rows

    def q_block(i, _):
        qs = pl.multiple_of(i * tq, tq)
        q = q_ref[0, pl.ds(qs, tq), :]
        fq = jnp.sum(jnp.where(lane == h, fcol_ref[0, pl.ds(qs, tq), :], 0.0),
                     axis=-1, keepdims=True)

        def step(j, carry, masked):
            m, l, acc = carry
            ks = pl.multiple_of(j * tq, tq)
            kj = k_ref[0, pl.ds(ks, tq), :]
            vj = v_ref[0, pl.ds(ks, tq), :]
            fk = frow_ref[0, 0, pl.ds(j, 1), :]
            a = lax.dot_general(q, kj, (((1,), (1,)), ((), ())),
                                preferred_element_type=_F32) - fk
            if masked:
                a = jnp.where(causal, a, -jnp.inf)
            m_new = jnp.maximum(m, jnp.max(a, axis=-1, keepdims=True) + fq)
            alpha = jnp.exp(m - m_new)
            p = jnp.exp(a - (m_new - fq))
            l = alpha * l + jnp.sum(p, axis=-1, keepdims=True)
            acc = alpha * acc + jnp.dot(p.astype(_BF16), vj, preferred_element_type=_F32)
            return m_new, l, acc

        init = (jnp.full((tq, 1), -jnp.inf, _F32), jnp.zeros((tq, 1), _F32),
                jnp.zeros((tq, HEAD_DIM), _F32))
        carry = lax.fori_loop(0, i, lambda j, cr: step(j, cr, False), init)
        _, l, acc = step(i, carry, True)
        o_ref[0, pl.ds(qs, tq), :] = (acc / l).astype(o_ref.dtype)
        return 0

    lax.fori_loop(0, n_blk, q_block, 0)


def _attention(q, k, v, fcol, frow4, *, tq=256):
    B, S, W = q.shape
    H = W // HEAD_DIM
    head = lambda b, h: (b, 0, h)
    return pl.pallas_call(
        functools.partial(_attn_kernel, tq=tq),
        grid=(B, H),
        in_specs=[pl.BlockSpec((1, S, HEAD_DIM), head),
                  pl.BlockSpec((1, S, HEAD_DIM), head),
                  pl.BlockSpec((1, S, HEAD_DIM), head),
                  pl.BlockSpec((1, S, LANES), lambda b, h: (b, 0, 0)),
                  pl.BlockSpec((1, 1, S // tq, tq), lambda b, h: (b, h, 0, 0))],
        out_specs=pl.BlockSpec((1, S, HEAD_DIM), head),
        out_shape=jax.ShapeDtypeStruct((B, S, W), _BF16),
        compiler_params=_params(("parallel", "parallel"), 32),
        name="attention",
    )(q, k, v, fcol, frow4)


def _sgu_kernel(u_ref, vn_ref, ws_ref, bs_ref, o_ref):
    tm, width = u_ref.shape
    n_groups = width // HEAD_DIM
    r = lax.broadcasted_iota(jnp.int32, (CHUNK, CHUNK), 0)
    cidx = lax.broadcasted_iota(jnp.int32, (CHUNK, CHUNK), 1)
    lower = cidx <= r
    for g in range(n_groups):
        w = jnp.where(lower, ws_ref[g], 0.0).astype(_BF16)
        bcol = jnp.broadcast_to(bs_ref[g:g + 1, :], (CHUNK, CHUNK)).T
        cs = slice(g * HEAD_DIM, (g + 1) * HEAD_DIM)
        for ci in range(tm // CHUNK):
            rs = slice(ci * CHUNK, (ci + 1) * CHUNK)
            z = jnp.dot(w, vn_ref[rs, cs], preferred_element_type=_F32) + bcol
            o_ref[rs, cs] = (u_ref[rs, cs].astype(_F32) * z).astype(o_ref.dtype)


def _sgu(u, vn, w_s, b_s, *, tm=512):
    M, width = u.shape
    G = w_s.shape[0]
    row = lambda m: (m, 0)
    return pl.pallas_call(
        _sgu_kernel,
        grid=(M // tm,),
        in_specs=[pl.BlockSpec((tm, width), row),
                  pl.BlockSpec((tm, width), row),
                  pl.BlockSpec((G, CHUNK, CHUNK), lambda m: (0, 0, 0)),
                  pl.BlockSpec((G, CHUNK), lambda m: (0, 0))],
        out_specs=pl.BlockSpec((tm, width), row),
        out_shape=jax.ShapeDtypeStruct((M, width), _BF16),
        compiler_params=_params(("parallel",), 32),
        name="sgu",
    )(u, vn, w_s, b_s)


def _outproj_kernel(a_ref, s_ref, w_ref, x_ref, mod_ref, o_ref):
    wa = a_ref.shape[1]
    mix = (jnp.dot(a_ref[...], w_ref[0:wa, :], preferred_element_type=_F32)
           + jnp.dot(s_ref[...], w_ref[wa:, :], preferred_element_type=_F32))
    o_ref[...] = x_ref[...] + mod_ref[0, _GATE1:_GATE1 + 1, :] * mix


def _outproj(attn, sgu, w_out, x2, mod3, *, seq, tm=512):
    M, D = x2.shape
    wa, ws = attn.shape[1], sgu.shape[1]
    tiles_per_seq = seq // tm
    row = lambda m: (m, 0)
    return pl.pallas_call(
        _outproj_kernel,
        grid=(M // tm,),
        in_specs=[pl.BlockSpec((tm, wa), row),
                  pl.BlockSpec((tm, ws), row),
                  pl.BlockSpec((wa + ws, D), lambda m: (0, 0)),
                  pl.BlockSpec((tm, D), row),
                  pl.BlockSpec((1, N_MOD, D), lambda m: (m // tiles_per_seq, 0, 0))],
        out_specs=pl.BlockSpec((tm, D), row),
        out_shape=jax.ShapeDtypeStruct((M, D), _F32),
        compiler_params=_params(("parallel",), 48),
        name="outproj",
    )(attn, sgu, w_out, x2, mod3)


def _ffn_kernel(x_ref, mod_ref, g_ref, w1_ref, w2_ref, o_ref, h_scr):
    f = pl.program_id(1)
    n_f = pl.num_programs(1)

    @pl.when(f == 0)
    def _():
        h = _norm_modulate(x_ref[...], g_ref[...],
                           mod_ref[0, _SHIFT2:_SHIFT2 + 1, :], mod_ref[0, _SCALE2:_SCALE2 + 1, :])
        h_scr[...] = h.astype(_BF16)

    hid = jnp.dot(h_scr[...], w1_ref[...], preferred_element_type=_F32)
    hid = jnp.square(jnp.maximum(hid, 0.0)).astype(_BF16)
    part = jnp.dot(hid, w2_ref[...], preferred_element_type=_F32)

    @pl.when(f == 0)
    def _():
        o_ref[...] = part

    @pl.when(f > 0)
    def _():
        o_ref[...] += part

    @pl.when(f == n_f - 1)
    def _():
        o_ref[...] = x_ref[...] + mod_ref[0, _GATE2:_GATE2 + 1, :] * o_ref[...]


def _ffn(x2, mod3, g_ffn, w1, w2, *, seq, tm=1024, tf=256):
    M, D = x2.shape
    F = w1.shape[1]
    tiles_per_seq = seq // tm
    row = lambda m, f: (m, 0)
    return pl.pallas_call(
        _ffn_kernel,
        grid=(M // tm, F // tf),
        in_specs=[pl.BlockSpec((tm, D), row),
                  pl.BlockSpec((1, N_MOD, D), lambda m, f: (m // tiles_per_seq, 0, 0)),
                  pl.BlockSpec((1, D), lambda m, f: (0, 0)),
                  pl.BlockSpec((D, tf), lambda m, f: (0, f)),
                  pl.BlockSpec((tf, D), lambda m, f: (f, 0))],
        out_specs=pl.BlockSpec((tm, D), row),
        out_shape=jax.ShapeDtypeStruct((M, D), _F32),
        scratch_shapes=[pltpu.VMEM((tm, D), _BF16)],
        compiler_params=_params(("parallel", "arbitrary"), 58),
        name="ffn",
    )(x2, mod3, g_ffn, w1, w2)


def kernel(x, c, w_ada, b_ada, g_mix, w_in, b_f, g_q, g_k, g_sgu, w_s, b_s, w_out, g_ffn,
           w_ff1, w_ff2):
    B, S, D = x.shape
    depth = w_ada.shape[0]
    H = b_f.shape[1]
    G = w_s.shape[1]
    attn_w = H * HEAD_DIM
    gmlp_w = G * HEAD_DIM
    tq = 256

    x2 = x.reshape(B * S, D)
    for l in range(depth):
        mod3 = _adaln(c, w_ada[l], b_ada[l]).reshape(B, N_MOD, D)

        wl = w_in[l]
        o_f = 3 * attn_w
        o_u = o_f + H
        w5 = jnp.stack([wl[:, 0:attn_w], wl[:, attn_w:2 * attn_w], wl[:, 2 * attn_w:o_f],
                        wl[:, o_u:o_u + gmlp_w], wl[:, o_u + gmlp_w:]]).astype(_BF16)
        w_f = jnp.pad(wl[:, o_f:o_u], ((0, 0), (0, LANES - H))).astype(_BF16)
        b_f_pad = jnp.pad(b_f[l], (0, LANES - H)).reshape(1, LANES)

        q, k, v, u, vn, flog = _inproj(
            x2, mod3, g_mix[l].reshape(1, D), w5, w_f, g_q[l].reshape(1, HEAD_DIM),
            g_k[l].reshape(1, HEAD_DIM), g_sgu[l].reshape(1, gmlp_w), seq=S)

        fcol, frow = _fcumsum(flog, b_f_pad, batch=B, seq=S, n_heads=H)
        attn = _attention(q.reshape(B, S, attn_w), k.reshape(B, S, attn_w),
                          v.reshape(B, S, attn_w), fcol.reshape(B, S, LANES),
                          frow.reshape(B, H, S // tq, tq), tq=tq)
        sgu = _sgu(u, vn, w_s[l], b_s[l])
        x2 = _outproj(attn.reshape(B * S, attn_w), sgu, w_out[l].astype(_BF16), x2, mod3, seq=S)
        x2 = _ffn(x2, mod3, g_ffn[l].reshape(1, D), w_ff1[l].astype(_BF16),
                  w_ff2[l].astype(_BF16), seq=S)
    return x2.reshape(B, S, D)
```

```python
import functools

import jax
import jax.numpy as jnp
import numpy as np
from jax import lax
from jax.experimental import pallas as pl
from jax.experimental.pallas import tpu as pltpu

_F32 = jnp.float32
_BF16 = jnp.bfloat16

HEAD_DIM = 128
CHUNK = 128
N_MOD = 6
EPS = 1e-6
LANES = 128

_SHIFT1, _SCALE1, _GATE1, _SHIFT2, _SCALE2, _GATE2 = range(6)

_MIB = 1024 * 1024


def _params(semantics, vmem_mib):
    return pltpu.CompilerParams(dimension_semantics=semantics,
                                vmem_limit_bytes=vmem_mib * _MIB)


def _adaln_kernel(c_ref, w_ref, b_ref, o_ref):
    c = c_ref[...]
    c_act = (c * jax.nn.sigmoid(c)).astype(_BF16)
    w = w_ref[...].astype(_BF16)
    o_ref[...] = jnp.dot(c_act, w, preferred_element_type=_F32) + b_ref[...]


def _adaln(c, w_ada, b_ada, *, tn=1024):
    B, D = c.shape
    N = w_ada.shape[1]
    return pl.pallas_call(
        _adaln_kernel,
        grid=(N // tn,),
        in_specs=[pl.BlockSpec((B, D), lambda n: (0, 0)),
                  pl.BlockSpec((D, tn), lambda n: (0, n)),
                  pl.BlockSpec((1, tn), lambda n: (0, n))],
        out_specs=pl.BlockSpec((B, tn), lambda n: (0, n)),
        out_shape=jax.ShapeDtypeStruct((B, N), _F32),
        compiler_params=_params(("parallel",), 40),
        name="adaln",
    )(c, w_ada, b_ada.reshape(1, N))


def _norm_modulate(x, g, shift, scale):
    ms = jnp.mean(x * x, axis=-1, keepdims=True)
    y = x * lax.rsqrt(ms + EPS) * g
    return y * (1.0 + scale) + shift


def _inproj_kernel(x_ref, mod_ref, gmix_ref, w_ref, wf_ref, gq_ref, gk_ref, gsgu_ref,
                   q_ref, k_ref, v_ref, u_ref, vn_ref, f_ref, h_scr, *, q_scale):
    n = pl.program_id(1)
    width = w_ref.shape[-1]
    n_heads = width // HEAD_DIM

    @pl.when(n == 0)
    def _():
        h = _norm_modulate(x_ref[...], gmix_ref[...],
                           mod_ref[0, _SHIFT1:_SHIFT1 + 1, :], mod_ref[0, _SCALE1:_SCALE1 + 1, :])
        hb = h.astype(_BF16)
        h_scr[...] = hb
        f_ref[...] = jnp.dot(hb, wf_ref[...], preferred_element_type=_F32)

    acc = jnp.dot(h_scr[...], w_ref[...], preferred_element_type=_F32)

    def head_rms(o_ref, g, scale):
        for hh in range(n_heads):
            sl = slice(hh * HEAD_DIM, (hh + 1) * HEAD_DIM)
            blk = acc[:, sl]
            ms = jnp.mean(blk * blk, axis=-1, keepdims=True)
            y = blk * lax.rsqrt(ms + EPS) * g
            if scale is not None:
                y = y * scale
            o_ref[:, sl] = y.astype(o_ref.dtype)

    @pl.when(n == 0)
    def _():
        head_rms(q_ref, gq_ref[...], q_scale)

    @pl.when(n == 1)
    def _():
        head_rms(k_ref, gk_ref[...], None)

    @pl.when(n == 2)
    def _():
        v_ref[...] = acc.astype(v_ref.dtype)

    @pl.when(n == 3)
    def _():
        u_ref[...] = jax.nn.gelu(acc).astype(u_ref.dtype)

    @pl.when(n == 4)
    def _():
        for gg in range(n_heads):
            sl = slice(gg * HEAD_DIM, (gg + 1) * HEAD_DIM)
            blk = jax.nn.gelu(acc[:, sl])
            mu = jnp.mean(blk, axis=-1, keepdims=True)
            xc = blk - mu
            var = jnp.mean(xc * xc, axis=-1, keepdims=True)
            y = xc * lax.rsqrt(var + EPS) * gsgu_ref[:, sl]
            vn_ref[:, sl] = y.astype(vn_ref.dtype)


def _inproj(x2, mod3, g_mix, w5, w_f, g_q, g_k, g_sgu, *, seq, tm=512):
    M, D = x2.shape
    n_sec, _, width = w5.shape
    tiles_per_seq = seq // tm
    row = lambda m, n: (m, 0)
    const2 = lambda m, n: (0, 0)
    act = jax.ShapeDtypeStruct((M, width), _BF16)
    return pl.pallas_call(
        functools.partial(_inproj_kernel, q_scale=np.float32(1.0 / np.sqrt(HEAD_DIM))),
        grid=(M // tm, n_sec),
        in_specs=[pl.BlockSpec((tm, D), row),
                  pl.BlockSpec((1, N_MOD, D), lambda m, n: (m // tiles_per_seq, 0, 0)),
                  pl.BlockSpec((1, D), const2),
                  pl.BlockSpec((None, D, width), lambda m, n: (n, 0, 0)),
                  pl.BlockSpec((D, LANES), const2),
                  pl.BlockSpec((1, HEAD_DIM), const2),
                  pl.BlockSpec((1, HEAD_DIM), const2),
                  pl.BlockSpec((1, width), const2)],
        out_specs=[pl.BlockSpec((tm, width), row)] * 5 + [pl.BlockSpec((tm, LANES), row)],
        out_shape=[act] * 5 + [jax.ShapeDtypeStruct((M, LANES), _F32)],
        scratch_shapes=[pltpu.VMEM((tm, D), _BF16)],
        compiler_params=_params(("parallel", "arbitrary"), 48),
        name="inproj",
    )(x2, mod3, g_mix, w5, w_f, g_q, g_k, g_sgu)


def _fcumsum_kernel(f_ref, bf_ref, fcol_ref, frow_ref, *, n_heads):
    S = f_ref.shape[0]
    r = lax.broadcasted_iota(jnp.int32, (CHUNK, CHUNK), 0)
    cidx = lax.broadcasted_iota(jnp.int32, (CHUNK, CHUNK), 1)
    tri = jnp.where(cidx <= r, 1.0, 0.0).astype(_BF16)
    carry = jnp.zeros((1, LANES), _F32)
    for ci in range(S // CHUNK):
        sl = slice(ci * CHUNK, (ci + 1) * CHUNK)
        lf = jax.nn.log_sigmoid(f_ref[sl, :] + bf_ref[...])
        hi = lf.astype(_BF16)
        r1 = lf - hi.astype(_F32)
        mid = r1.astype(_BF16)
        lo = (r1 - mid.astype(_F32)).astype(_BF16)
        cs = (jnp.dot(tri, hi, preferred_element_type=_F32)
              + jnp.dot(tri, mid, preferred_element_type=_F32)
              + jnp.dot(tri, lo, preferred_element_type=_F32))
        F = cs + carry
        fcol_ref[sl, :] = F
        frow_ref[0, :, sl] = F.T[0:n_heads, :]
        carry = F[CHUNK - 1:CHUNK, :]


def _fcumsum(flog, b_f_pad, *, batch, seq, n_heads):
    return pl.pallas_call(
        functools.partial(_fcumsum_kernel, n_heads=n_heads),
        grid=(batch,),
        in_specs=[pl.BlockSpec((seq, LANES), lambda b: (b, 0)),
                  pl.BlockSpec((1, LANES), lambda b: (0, 0))],
        out_specs=[pl.BlockSpec((seq, LANES), lambda b: (b, 0)),
                   pl.BlockSpec((1, n_heads, seq), lambda b: (b, 0, 0))],
        out_shape=[jax.ShapeDtypeStruct((batch * seq, LANES), _F32),
                   jax.ShapeDtypeStruct((batch, n_heads, seq), _F32)],
        compiler_params=_params(("parallel",), 32),
        name="fcumsum",
    )(flog, b_f_pad)


def _attn_kernel(q_ref, k_ref, v_ref, fcol_ref, frow_ref, o_ref, *, tq, hps):
    hp = pl.program_id(1)
    n_blk = q_ref.shape[1] // tq
    lane = lax.broadcasted_iota(jnp.int32, (tq, LANES), 1)
    rows = lax.broadcasted_iota(jnp.int32, (tq, tq), 0)
    cols = lax.broadcasted_iota(jnp.int32, (tq, tq), 1)
    causal = cols <= rows
    hsl = [slice(hh * HEAD_DIM, (hh + 1) * HEAD_DIM) for hh in range(hps)]

    def q_block(i, _):
        qs = pl.multiple_of(i * tq, tq)
        fc = fcol_ref[0, pl.ds(qs, tq), :]
        q = [q_ref[0, pl.ds(qs, tq), hsl[hh]] for hh in range(hps)]
        fq = [jnp.sum(jnp.where(lane == hp * hps + hh, fc, 0.0), axis=-1, keepdims=True)
              for hh in range(hps)]

        def step(j, carry, masked):
            ks = pl.multiple_of(j * tq, tq)
            out = []
            for hh in range(hps):
                m, l, acc = carry[hh]
                kj = k_ref[0, pl.ds(ks, tq), hsl[hh]]
                vj = v_ref[0, pl.ds(ks, tq), hsl[hh]]
                fk = frow_ref[0, hh, pl.ds(j, 1), :]
                a = lax.dot_general(q[hh], kj, (((1,), (1,)), ((), ())),
                                    preferred_element_type=_F32) - fk
                if masked:
                    a = jnp.where(causal, a, -jnp.inf)
                m_new = jnp.maximum(m, jnp.max(a, axis=-1, keepdims=True) + fq[hh])
                alpha = jnp.exp(m - m_new)
                p = jnp.exp(a - (m_new - fq[hh]))
                l = alpha * l + jnp.sum(p, axis=-1, keepdims=True)
                acc = alpha * acc + jnp.dot(p.astype(_BF16), vj, preferred_element_type=_F32)
                out.append((m_new, l, acc))
            return tuple(out)

        init = tuple((jnp.full((tq, 1), -jnp.inf, _F32), jnp.zeros((tq, 1), _F32),
                      jnp.zeros((tq, HEAD_DIM), _F32)) for _ in range(hps))
        carry = lax.fori_loop(0, i, lambda j, cr: step(j, cr, False), init)
        final = step(i, carry, True)
        for hh in range(hps):
            _, l, acc = final[hh]
            o_ref[0, pl.ds(qs, tq), hsl[hh]] = (acc / l).astype(o_ref.dtype)
        return 0

    lax.fori_loop(0, n_blk, q_block, 0)


def _attention(q, k, v, fcol, frow4, *, tq, hps=2):
    B, S, W = q.shape
    H = W // HEAD_DIM
    heads = lambda b, h: (b, 0, h)
    return pl.pallas_call(
        functools.partial(_attn_kernel, tq=tq, hps=hps),
        grid=(B, H // hps),
        in_specs=[pl.BlockSpec((1, S, hps * HEAD_DIM), heads),
                  pl.BlockSpec((1, S, hps * HEAD_DIM), heads),
                  pl.BlockSpec((1, S, hps * HEAD_DIM), heads),
                  pl.BlockSpec((1, S, LANES), lambda b, h: (b, 0, 0)),
                  pl.BlockSpec((1, hps, S // tq, tq), lambda b, h: (b, h, 0, 0))],
        out_specs=pl.BlockSpec((1, S, hps * HEAD_DIM), heads),
        out_shape=jax.ShapeDtypeStruct((B, S, W), _BF16),
        compiler_params=_params(("parallel", "parallel"), 40),
        name="attention",
    )(q, k, v, fcol, frow4)


def _sgu_kernel(u_ref, vn_ref, ws_ref, bs_ref, o_ref):
    tm, width = u_ref.shape
    n_groups = width // HEAD_DIM
    r = lax.broadcasted_iota(jnp.int32, (CHUNK, CHUNK), 0)
    cidx = lax.broadcasted_iota(jnp.int32, (CHUNK, CHUNK), 1)
    lower = cidx <= r
    for g in range(n_groups):
        w = jnp.where(lower, ws_ref[g], 0.0).astype(_BF16)
        bcol = jnp.broadcast_to(bs_ref[g:g + 1, :], (CHUNK, CHUNK)).T
        cs = slice(g * HEAD_DIM, (g + 1) * HEAD_DIM)
        for ci in range(tm // CHUNK):
            rs = slice(ci * CHUNK, (ci + 1) * CHUNK)
            z = jnp.dot(w, vn_ref[rs, cs], preferred_element_type=_F32) + bcol
            o_ref[rs, cs] = (u_ref[rs, cs].astype(_F32) * z).astype(o_ref.dtype)


def _sgu(u, vn, w_s, b_s, *, tm=512):
    M, width = u.shape
    G = w_s.shape[0]
    row = lambda m: (m, 0)
    return pl.pallas_call(
        _sgu_kernel,
        grid=(M // tm,),
        in_specs=[pl.BlockSpec((tm, width), row),
                  pl.BlockSpec((tm, width), row),
                  pl.BlockSpec((G, CHUNK, CHUNK), lambda m: (0, 0, 0)),
                  pl.BlockSpec((G, CHUNK), lambda m: (0, 0))],
        out_specs=pl.BlockSpec((tm, width), row),
        out_shape=jax.ShapeDtypeStruct((M, width), _BF16),
        compiler_params=_params(("parallel",), 32),
        name="sgu",
    )(u, vn, w_s, b_s)


def _outproj_kernel(a_ref, s_ref, w_ref, x_ref, mod_ref, o_ref):
    wa = a_ref.shape[1]
    mix = (jnp.dot(a_ref[...], w_ref[0:wa, :], preferred_element_type=_F32)
           + jnp.dot(s_ref[...], w_ref[wa:, :], preferred_element_type=_F32))
    o_ref[...] = x_ref[...] + mod_ref[0, _GATE1:_GATE1 + 1, :] * mix


def _outproj(attn, sgu, w_out, x2, mod3, *, seq, tm=512):
    M, D = x2.shape
    wa, ws = attn.shape[1], sgu.shape[1]
    tiles_per_seq = seq // tm
    row = lambda m: (m, 0)
    return pl.pallas_call(
        _outproj_kernel,
        grid=(M // tm,),
        in_specs=[pl.BlockSpec((tm, wa), row),
                  pl.BlockSpec((tm, ws), row),
                  pl.BlockSpec((wa + ws, D), lambda m: (0, 0)),
                  pl.BlockSpec((tm, D), row),
                  pl.BlockSpec((1, N_MOD, D), lambda m: (m // tiles_per_seq, 0, 0))],
        out_specs=pl.BlockSpec((tm, D), row),
        out_shape=jax.ShapeDtypeStruct((M, D), _F32),
        compiler_params=_params(("parallel",), 48),
        name="outproj",
    )(attn, sgu, w_out, x2, mod3)


def _ffn_kernel(x_ref, mod_ref, g_ref, w1_ref, w2_ref, o_ref, h_scr):
    f = pl.program_id(1)

    @pl.when(f == 0)
    def _():
        x = x_ref[...]
        h = _norm_modulate(x, g_ref[...],
                           mod_ref[0, _SHIFT2:_SHIFT2 + 1, :], mod_ref[0, _SCALE2:_SCALE2 + 1, :])
        h_scr[...] = h.astype(_BF16)
        o_ref[...] = x

    hid = jnp.dot(h_scr[...], w1_ref[...], preferred_element_type=_F32)
    hid = jnp.square(jnp.maximum(hid, 0.0)).astype(_BF16)
    part = jnp.dot(hid, w2_ref[...], preferred_element_type=_F32)
    o_ref[...] += mod_ref[0, _GATE2:_GATE2 + 1, :] * part


def _ffn(x2, mod3, g_ffn, w1, w2, *, seq, tm=512, tf=1024):
    M, D = x2.shape
    F = w1.shape[1]
    tiles_per_seq = seq // tm
    row = lambda m, f: (m, 0)
    return pl.pallas_call(
        _ffn_kernel,
        grid=(M // tm, F // tf),
        in_specs=[pl.BlockSpec((tm, D), row),
                  pl.BlockSpec((1, N_MOD, D), lambda m, f: (m // tiles_per_seq, 0, 0)),
                  pl.BlockSpec((1, D), lambda m, f: (0, 0)),
                  pl.BlockSpec((D, tf), lambda m, f: (0, f)),
                  pl.BlockSpec((tf, D), lambda m, f: (f, 0))],
        out_specs=pl.BlockSpec((tm, D), row),
        out_shape=jax.ShapeDtypeStruct((M, D), _F32),
        scratch_shapes=[pltpu.VMEM((tm, D), _BF16)],
        compiler_params=_params(("parallel", "arbitrary"), 52),
        name="ffn",
    )(x2, mod3, g_ffn, w1, w2)


def kernel(x, c, w_ada, b_ada, g_mix, w_in, b_f, g_q, g_k, g_sgu, w_s, b_s, w_out, g_ffn,
           w_ff1, w_ff2):
    B, S, D = x.shape
    depth = w_ada.shape[0]
    H = b_f.shape[1]
    G = w_s.shape[1]
    attn_w = H * HEAD_DIM
    gmlp_w = G * HEAD_DIM
    tq = 512

    x2 = x.reshape(B * S, D)
    for l in range(depth):
        mod3 = _adaln(c, w_ada[l], b_ada[l]).reshape(B, N_MOD, D)

        wl = w_in[l]
        o_f = 3 * attn_w
        o_u = o_f + H
        w5 = jnp.stack([wl[:, 0:attn_w], wl[:, attn_w:2 * attn_w], wl[:, 2 * attn_w:o_f],
                        wl[:, o_u:o_u + gmlp_w], wl[:, o_u + gmlp_w:]]).astype(_BF16)
        w_f = jnp.pad(wl[:, o_f:o_u], ((0, 0), (0, LANES - H))).astype(_BF16)
        b_f_pad = jnp.pad(b_f[l], (0, LANES - H)).reshape(1, LANES)

        q, k, v, u, vn, flog = _inproj(
            x2, mod3, g_mix[l].reshape(1, D), w5, w_f, g_q[l].reshape(1, HEAD_DIM),
            g_k[l].reshape(1, HEAD_DIM), g_sgu[l].reshape(1, gmlp_w), seq=S)

        fcol, frow = _fcumsum(flog, b_f_pad, batch=B, seq=S, n_heads=H)
        attn = _attention(q.reshape(B, S, attn_w), k.reshape(B, S, attn_w),
                          v.reshape(B, S, attn_w), fcol.reshape(B, S, LANES),
                          frow.reshape(B, H, S // tq, tq), tq=tq)
        sgu = _sgu(u, vn, w_s[l], b_s[l])
        x2 = _outproj(attn.reshape(B * S, attn_w), sgu, w_out[l].astype(_BF16), x2, mod3, seq=S)
        x2 = _ffn(x2, mod3, g_ffn[l].reshape(1, D), w_ff1[l].astype(_BF16),
                  w_ff2[l].astype(_BF16), seq=S)
    return x2.reshape(B, S, D)
```

```python
import functools

import jax
import jax.numpy as jnp
import numpy as np
from jax import lax
from jax.experimental import pallas as pl
from jax.experimental.pallas import tpu as pltpu

_F32 = jnp.float32
_BF16 = jnp.bfloat16

HEAD_DIM = 128
CHUNK = 128
N_MOD = 6
EPS = 1e-6
LANES = 128
_LOG2E = np.float32(np.log2(np.e))

_SHIFT1, _SCALE1, _GATE1, _SHIFT2, _SCALE2, _GATE2 = range(6)

_MIB = 1024 * 1024


def _params(semantics, vmem_mib):
    return pltpu.CompilerParams(dimension_semantics=semantics,
                                vmem_limit_bytes=vmem_mib * _MIB)


def _adaln_kernel(c_ref, w_ref, b_ref, o_ref):
    c = c_ref[...]
    c_act = (c * jax.nn.sigmoid(c)).astype(_BF16)
    w = w_ref[...].astype(_BF16)
    o_ref[...] = jnp.dot(c_act, w, preferred_element_type=_F32) + b_ref[...]


def _adaln(c, w_ada, b_ada, *, tn=1024):
    B, D = c.shape
    N = w_ada.shape[1]
    return pl.pallas_call(
        _adaln_kernel,
        grid=(N // tn,),
        in_specs=[pl.BlockSpec((B, D), lambda n: (0, 0)),
                  pl.BlockSpec((D, tn), lambda n: (0, n)),
                  pl.BlockSpec((1, tn), lambda n: (0, n))],
        out_specs=pl.BlockSpec((B, tn), lambda n: (0, n)),
        out_shape=jax.ShapeDtypeStruct((B, N), _F32),
        compiler_params=_params(("parallel",), 40),
        name="adaln",
    )(c, w_ada, b_ada.reshape(1, N))


def _norm_modulate(x, g, shift, scale):
    ms = jnp.mean(x * x, axis=-1, keepdims=True)
    y = x * lax.rsqrt(ms + EPS) * g
    return y * (1.0 + scale) + shift


def _inproj_kernel(x_ref, mod_ref, gmix_ref, w_ref, wf_ref, gq_ref, gk_ref, gsgu_ref,
                   q_ref, k_ref, v_ref, u_ref, vn_ref, f_ref, h_scr, *, q_scale):
    width = w_ref.shape[-1]
    n_heads = width // HEAD_DIM
    sec_q, sec_k, sec_v, sec_u, sec_vg = range(5)

    h = _norm_modulate(x_ref[...], gmix_ref[...],
                       mod_ref[0, _SHIFT1:_SHIFT1 + 1, :], mod_ref[0, _SCALE1:_SCALE1 + 1, :])
    h_scr[...] = h.astype(_BF16)
    f_ref[...] = jnp.dot(h_scr[...], wf_ref[...], preferred_element_type=_F32)

    def section(idx):
        return jnp.dot(h_scr[...], w_ref[idx], preferred_element_type=_F32)

    def head_rms(acc, o_ref, g, scale):
        for hh in range(n_heads):
            sl = slice(hh * HEAD_DIM, (hh + 1) * HEAD_DIM)
            blk = acc[:, sl]
            ms = jnp.mean(blk * blk, axis=-1, keepdims=True)
            y = blk * lax.rsqrt(ms + EPS) * g
            if scale is not None:
                y = y * scale
            o_ref[:, sl] = y.astype(o_ref.dtype)

    acc = section(sec_vg)
    for gg in range(n_heads):
        sl = slice(gg * HEAD_DIM, (gg + 1) * HEAD_DIM)
        blk = jax.nn.gelu(acc[:, sl])
        mu = jnp.mean(blk, axis=-1, keepdims=True)
        xc = blk - mu
        var = jnp.mean(xc * xc, axis=-1, keepdims=True)
        y = xc * lax.rsqrt(var + EPS) * gsgu_ref[:, sl]
        vn_ref[:, sl] = y.astype(vn_ref.dtype)

    head_rms(section(sec_q), q_ref, gq_ref[...], q_scale)
    head_rms(section(sec_k), k_ref, gk_ref[...], None)
    u_ref[...] = jax.nn.gelu(section(sec_u)).astype(u_ref.dtype)
    v_ref[...] = section(sec_v).astype(v_ref.dtype)


def _inproj(x2, mod3, g_mix, w5, w_f, g_q, g_k, g_sgu, *, seq, tm=512):
    M, D = x2.shape
    n_sec, _, width = w5.shape
    tiles_per_seq = seq // tm
    row = lambda m: (m, 0)
    const2 = lambda m: (0, 0)
    resident = pl.Buffered(1)
    act = jax.ShapeDtypeStruct((M, width), _BF16)
    return pl.pallas_call(
        functools.partial(_inproj_kernel, q_scale=np.float32(_LOG2E / np.sqrt(HEAD_DIM))),
        grid=(M // tm,),
        in_specs=[pl.BlockSpec((tm, D), row),
                  pl.BlockSpec((1, N_MOD, D), lambda m: (m // tiles_per_seq, 0, 0)),
                  pl.BlockSpec((1, D), const2),
                  pl.BlockSpec((n_sec, D, width), lambda m: (0, 0, 0), pipeline_mode=resident),
                  pl.BlockSpec((D, LANES), const2, pipeline_mode=resident),
                  pl.BlockSpec((1, HEAD_DIM), const2),
                  pl.BlockSpec((1, HEAD_DIM), const2),
                  pl.BlockSpec((1, width), const2)],
        out_specs=[pl.BlockSpec((tm, width), row)] * 5 + [pl.BlockSpec((tm, LANES), row)],
        out_shape=[act] * 5 + [jax.ShapeDtypeStruct((M, LANES), _F32)],
        scratch_shapes=[pltpu.VMEM((tm, D), _BF16)],
        compiler_params=_params(("parallel",), 56),
        name="inproj",
    )(x2, mod3, g_mix, w5, w_f, g_q, g_k, g_sgu)


def _fcumsum_kernel(f_ref, bf_ref, fcol_ref, frow_ref, *, n_heads):
    S = f_ref.shape[0]
    r = lax.broadcasted_iota(jnp.int32, (CHUNK, CHUNK), 0)
    cidx = lax.broadcasted_iota(jnp.int32, (CHUNK, CHUNK), 1)
    tri = jnp.where(cidx <= r, 1.0, 0.0).astype(_BF16)
    carry = jnp.zeros((1, LANES), _F32)
    for ci in range(S // CHUNK):
        sl = slice(ci * CHUNK, (ci + 1) * CHUNK)
        lf = jax.nn.log_sigmoid(f_ref[sl, :] + bf_ref[...])
        hi = lf.astype(_BF16)
        r1 = lf - hi.astype(_F32)
        mid = r1.astype(_BF16)
        lo = (r1 - mid.astype(_F32)).astype(_BF16)
        cs = (jnp.dot(tri, hi, preferred_element_type=_F32)
              + jnp.dot(tri, mid, preferred_element_type=_F32)
              + jnp.dot(tri, lo, preferred_element_type=_F32))
        F = cs + carry
        F2 = F * _LOG2E
        fcol_ref[sl, :] = F2
        frow_ref[0, :, sl] = F2.T[0:n_heads, :]
        carry = F[CHUNK - 1:CHUNK, :]


def _fcumsum(flog, b_f_pad, *, batch, seq, n_heads):
    return pl.pallas_call(
        functools.partial(_fcumsum_kernel, n_heads=n_heads),
        grid=(batch,),
        in_specs=[pl.BlockSpec((seq, LANES), lambda b: (b, 0)),
                  pl.BlockSpec((1, LANES), lambda b: (0, 0))],
        out_specs=[pl.BlockSpec((seq, LANES), lambda b: (b, 0)),
                   pl.BlockSpec((1, n_heads, seq), lambda b: (b, 0, 0))],
        out_shape=[jax.ShapeDtypeStruct((batch * seq, LANES), _F32),
                   jax.ShapeDtypeStruct((batch, n_heads, seq), _F32)],
        compiler_params=_params(("parallel",), 32),
        name="fcumsum",
    )(flog, b_f_pad)


def _attn_kernel(q_ref, k_ref, v_ref, fcol_ref, frow_ref, o_ref, v2_scr, s_scr, *, tq, hps):
    hp = pl.program_id(1)
    S = q_ref.shape[1]
    n_blk = S // tq
    hsl = [slice(hh * HEAD_DIM, (hh + 1) * HEAD_DIM) for hh in range(hps)]
    den = slice(HEAD_DIM, 2 * HEAD_DIM)
    tiles = [(i, j) for i in range(n_blk) for j in range(i + 1)]

    for hh in range(hps):
        v2_scr[hh, :, 0:HEAD_DIM] = v_ref[0, :, hsl[hh]]
        v2_scr[hh, :, den] = jnp.ones((S, HEAD_DIM), _BF16)

    lane = lax.broadcasted_iota(jnp.int32, (tq, LANES), 1)
    rows = lax.broadcasted_iota(jnp.int32, (tq, tq), 0)
    cols = lax.broadcasted_iota(jnp.int32, (tq, tq), 1)
    causal = cols <= rows

    def scores(t):
        i, j = tiles[t]
        for hh in range(hps):
            s = lax.dot_general(q_ref[0, i * tq:(i + 1) * tq, hsl[hh]],
                                k_ref[0, j * tq:(j + 1) * tq, hsl[hh]],
                                (((1,), (1,)), ((), ())), preferred_element_type=_F32)
            s_scr[t % 2, hh] = s - frow_ref[0, 0, hh:hh + 1, j * tq:(j + 1) * tq]

    scores(0)
    fq = m = acc = None
    for t, (i, j) in enumerate(tiles):
        if t + 1 < len(tiles):
            scores(t + 1)
        if j == 0:
            fc = fcol_ref[0, i * tq:(i + 1) * tq, :]
            fq = [jnp.sum(jnp.where(lane == hp * hps + hh, fc, 0.0), axis=-1, keepdims=True)
                  for hh in range(hps)]
            m = [None] * hps
            acc = [None] * hps
        for hh in range(hps):
            a = s_scr[t % 2, hh]
            if j == i:
                a = jnp.where(causal, a, -jnp.inf)
            m_tile = jnp.max(a, axis=-1, keepdims=True) + fq[hh]
            m_new = m_tile if j == 0 else jnp.maximum(m[hh], m_tile)
            p = jnp.exp2(a - (m_new - fq[hh])).astype(_BF16)
            pv = jnp.dot(p, v2_scr[hh, j * tq:(j + 1) * tq, :], preferred_element_type=_F32)
            acc[hh] = pv if j == 0 else jnp.exp2(m[hh] - m_new) * acc[hh] + pv
            m[hh] = m_new
            if j == i:
                o_ref[0, i * tq:(i + 1) * tq, hsl[hh]] = (
                    acc[hh][:, 0:HEAD_DIM] / acc[hh][:, den]).astype(o_ref.dtype)


def _attention(q, k, v, fcol, frow, *, tq=512, hps=2):
    B, S, W = q.shape
    H = W // HEAD_DIM
    heads = lambda b, h: (b, 0, h)
    return pl.pallas_call(
        functools.partial(_attn_kernel, tq=tq, hps=hps),
        grid=(B, H // hps),
        in_specs=[pl.BlockSpec((1, S, hps * HEAD_DIM), heads),
                  pl.BlockSpec((1, S, hps * HEAD_DIM), heads),
                  pl.BlockSpec((1, S, hps * HEAD_DIM), heads),
                  pl.BlockSpec((1, S, LANES), lambda b, h: (b, 0, 0)),
                  pl.BlockSpec((1, 1, hps, S), lambda b, h: (b, h, 0, 0))],
        out_specs=pl.BlockSpec((1, S, hps * HEAD_DIM), heads),
        out_shape=jax.ShapeDtypeStruct((B, S, W), _BF16),
        scratch_shapes=[pltpu.VMEM((hps, S, 2 * HEAD_DIM), _BF16),
                        pltpu.VMEM((2, hps, tq, tq), _F32)],
        compiler_params=_params(("parallel", "parallel"), 40),
        name="attention",
    )(q, k, v, fcol, frow.reshape(B, H // hps, hps, S))


def _sgu_kernel(u_ref, vn_ref, ws_ref, bs_ref, o_ref):
    tm, width = u_ref.shape
    n_groups = width // HEAD_DIM
    r = lax.broadcasted_iota(jnp.int32, (CHUNK, CHUNK), 0)
    cidx = lax.broadcasted_iota(jnp.int32, (CHUNK, CHUNK), 1)
    lower = cidx <= r
    for g in range(n_groups):
        w = jnp.where(lower, ws_ref[g], 0.0).astype(_BF16)
        bcol = jnp.broadcast_to(bs_ref[g:g + 1, :], (CHUNK, CHUNK)).T
        cs = slice(g * HEAD_DIM, (g + 1) * HEAD_DIM)
        for ci in range(tm // CHUNK):
            rs = slice(ci * CHUNK, (ci + 1) * CHUNK)
            z = jnp.dot(w, vn_ref[rs, cs], preferred_element_type=_F32) + bcol
            o_ref[rs, cs] = (u_ref[rs, cs].astype(_F32) * z).astype(o_ref.dtype)


def _sgu(u, vn, w_s, b_s, *, tm=512):
    M, width = u.shape
    G = w_s.shape[0]
    row = lambda m: (m, 0)
    return pl.pallas_call(
        _sgu_kernel,
        grid=(M // tm,),
        in_specs=[pl.BlockSpec((tm, width), row),
                  pl.BlockSpec((tm, width), row),
                  pl.BlockSpec((G, CHUNK, CHUNK), lambda m: (0, 0, 0)),
                  pl.BlockSpec((G, CHUNK), lambda m: (0, 0))],
        out_specs=pl.BlockSpec((tm, width), row),
        out_shape=jax.ShapeDtypeStruct((M, width), _BF16),
        compiler_params=_params(("parallel",), 32),
        name="sgu",
    )(u, vn, w_s, b_s)


def _outproj_kernel(a_ref, s_ref, w_ref, x_ref, mod_ref, o_ref):
    wa = a_ref.shape[1]
    mix = (jnp.dot(a_ref[...], w_ref[0:wa, :], preferred_element_type=_F32)
           + jnp.dot(s_ref[...], w_ref[wa:, :], preferred_element_type=_F32))
    o_ref[...] = x_ref[...] + mod_ref[0, _GATE1:_GATE1 + 1, :] * mix


def _outproj(attn, sgu, w_out, x2, mod3, *, seq, tm=512):
    M, D = x2.shape
    wa, ws = attn.shape[1], sgu.shape[1]
    tiles_per_seq = seq // tm
    row = lambda m: (m, 0)
    return pl.pallas_call(
        _outproj_kernel,
        grid=(M // tm,),
        in_specs=[pl.BlockSpec((tm, wa), row),
                  pl.BlockSpec((tm, ws), row),
                  pl.BlockSpec((wa + ws, D), lambda m: (0, 0)),
                  pl.BlockSpec((tm, D), row),
                  pl.BlockSpec((1, N_MOD, D), lambda m: (m // tiles_per_seq, 0, 0))],
        out_specs=pl.BlockSpec((tm, D), row),
        out_shape=jax.ShapeDtypeStruct((M, D), _F32),
        compiler_params=_params(("parallel",), 48),
        name="outproj",
    )(attn, sgu, w_out, x2, mod3)


def _ffn_kernel(x_ref, mod_ref, g_ref, w1_ref, w2_ref, o_ref, h_scr):
    f = pl.program_id(1)

    @pl.when(f == 0)
    def _():
        x = x_ref[...]
        h = _norm_modulate(x, g_ref[...],
                           mod_ref[0, _SHIFT2:_SHIFT2 + 1, :], mod_ref[0, _SCALE2:_SCALE2 + 1, :])
        h_scr[...] = h.astype(_BF16)
        o_ref[...] = x

    hid = jnp.dot(h_scr[...], w1_ref[...], preferred_element_type=_F32)
    hid = jnp.square(jnp.maximum(hid, 0.0)).astype(_BF16)
    part = jnp.dot(hid, w2_ref[...], preferred_element_type=_F32)
    o_ref[...] += mod_ref[0, _GATE2:_GATE2 + 1, :] * part


def _ffn(x2, mod3, g_ffn, w1, w2, *, seq, tm=512, tf=1024):
    M, D = x2.shape
    F = w1.shape[1]
    tiles_per_seq = seq // tm
    row = lambda m, f: (m, 0)
    return pl.pallas_call(
        _ffn_kernel,
        grid=(M // tm, F // tf),
        in_specs=[pl.BlockSpec((tm, D), row),
                  pl.BlockSpec((1, N_MOD, D), lambda m, f: (m // tiles_per_seq, 0, 0)),
                  pl.BlockSpec((1, D), lambda m, f: (0, 0)),
                  pl.BlockSpec((D, tf), lambda m, f: (0, f)),
                  pl.BlockSpec((tf, D), lambda m, f: (f, 0))],
        out_specs=pl.BlockSpec((tm, D), row),
        out_shape=jax.ShapeDtypeStruct((M, D), _F32),
        scratch_shapes=[pltpu.VMEM((tm, D), _BF16)],
        compiler_params=_params(("parallel", "arbitrary"), 52),
        name="ffn",
    )(x2, mod3, g_ffn, w1, w2)


def kernel(x, c, w_ada, b_ada, g_mix, w_in, b_f, g_q, g_k, g_sgu, w_s, b_s, w_out, g_ffn,
           w_ff1, w_ff2):
    B, S, D = x.shape
    depth = w_ada.shape[0]
    H = b_f.shape[1]
    G = w_s.shape[1]
    attn_w = H * HEAD_DIM
    gmlp_w = G * HEAD_DIM
    x2 = x.reshape(B * S, D)
    for l in range(depth):
        mod3 = _adaln(c, w_ada[l], b_ada[l]).reshape(B, N_MOD, D)

        wl = w_in[l]
        o_f = 3 * attn_w
        o_u = o_f + H
        w5 = jnp.stack([wl[:, 0:attn_w], wl[:, attn_w:2 * attn_w], wl[:, 2 * attn_w:o_f],
                        wl[:, o_u:o_u + gmlp_w], wl[:, o_u + gmlp_w:]]).astype(_BF16)
        w_f = jnp.pad(wl[:, o_f:o_u], ((0, 0), (0, LANES - H))).astype(_BF16)
        b_f_pad = jnp.pad(b_f[l], (0, LANES - H)).reshape(1, LANES)

        q, k, v, u, vn, flog = _inproj(
            x2, mod3, g_mix[l].reshape(1, D), w5, w_f, g_q[l].reshape(1, HEAD_DIM),
            g_k[l].reshape(1, HEAD_DIM), g_sgu[l].reshape(1, gmlp_w), seq=S)

        fcol, frow = _fcumsum(flog, b_f_pad, batch=B, seq=S, n_heads=H)
        attn = _attention(q.reshape(B, S, attn_w), k.reshape(B, S, attn_w),
                          v.reshape(B, S, attn_w), fcol.reshape(B, S, LANES), frow)
        sgu = _sgu(u, vn, w_s[l], b_s[l])
        x2 = _outproj(attn.reshape(B * S, attn_w), sgu, w_out[l].astype(_BF16), x2, mod3, seq=S)
        x2 = _ffn(x2, mod3, g_ffn[l].reshape(1, D), w_ff1[l].astype(_BF16),
                  w_ff2[l].astype(_BF16), seq=S)
    return x2.reshape(B, S, D)
```

```python
import functools

import jax
import jax.numpy as jnp
import numpy as np
from jax import lax
from jax.experimental import pallas as pl
from jax.experimental.pallas import tpu as pltpu

_F32 = jnp.float32
_BF16 = jnp.bfloat16

HEAD_DIM = 128
CHUNK = 128
N_MOD = 6
EPS = 1e-6
LANES = 128
_LOG2E = np.float32(np.log2(np.e))

_SHIFT1, _SCALE1, _GATE1, _SHIFT2, _SCALE2, _GATE2 = range(6)

_MIB = 1024 * 1024


def _params(semantics, vmem_mib):
    return pltpu.CompilerParams(dimension_semantics=semantics,
                                vmem_limit_bytes=vmem_mib * _MIB)


def _adaln_kernel(c_ref, w_ref, b_ref, o_ref):
    c = c_ref[...]
    c_act = (c * jax.nn.sigmoid(c)).astype(_BF16)
    w = w_ref[...].astype(_BF16)
    o_ref[...] = jnp.dot(c_act, w, preferred_element_type=_F32) + b_ref[...]


def _adaln(c, w_ada, b_ada, *, tn=1024):
    B, D = c.shape
    N = w_ada.shape[1]
    return pl.pallas_call(
        _adaln_kernel,
        grid=(N // tn,),
        in_specs=[pl.BlockSpec((B, D), lambda n: (0, 0)),
                  pl.BlockSpec((D, tn), lambda n: (0, n)),
                  pl.BlockSpec((1, tn), lambda n: (0, n))],
        out_specs=pl.BlockSpec((B, tn), lambda n: (0, n)),
        out_shape=jax.ShapeDtypeStruct((B, N), _F32),
        compiler_params=_params(("parallel",), 40),
        name="adaln",
    )(c, w_ada, b_ada.reshape(1, N))


def _norm_modulate(x, g, shift, scale):
    ms = jnp.mean(x * x, axis=-1, keepdims=True)
    y = x * lax.rsqrt(ms + EPS) * g
    return y * (1.0 + scale) + shift


def _inproj_kernel(x_ref, mod_ref, gmix_ref, w_ref, wf_ref, gq_ref, gk_ref, gsgu_ref,
                   q_ref, k_ref, v_ref, u_ref, vn_ref, f_ref, h_scr, *, q_scale):
    width = q_ref.shape[-1]
    n_heads = width // HEAD_DIM
    sec_q, sec_k, sec_v, sec_u, sec_vg = range(5)

    h = _norm_modulate(x_ref[...], gmix_ref[...],
                       mod_ref[0, _SHIFT1:_SHIFT1 + 1, :], mod_ref[0, _SCALE1:_SCALE1 + 1, :])
    h_scr[...] = h.astype(_BF16)
    f_ref[...] = jnp.dot(h_scr[...], wf_ref[...], preferred_element_type=_F32)

    def section(idx):
        return jnp.dot(h_scr[...], w_ref[:, idx * width:(idx + 1) * width],
                       preferred_element_type=_F32)

    def head_rms(acc, o_ref, g, scale):
        for hh in range(n_heads):
            sl = slice(hh * HEAD_DIM, (hh + 1) * HEAD_DIM)
            blk = acc[:, sl]
            ms = jnp.mean(blk * blk, axis=-1, keepdims=True)
            y = blk * lax.rsqrt(ms + EPS) * g
            if scale is not None:
                y = y * scale
            o_ref[:, sl] = y.astype(o_ref.dtype)

    acc = section(sec_vg)
    for gg in range(n_heads):
        sl = slice(gg * HEAD_DIM, (gg + 1) * HEAD_DIM)
        blk = jax.nn.gelu(acc[:, sl])
        mu = jnp.mean(blk, axis=-1, keepdims=True)
        xc = blk - mu
        var = jnp.mean(xc * xc, axis=-1, keepdims=True)
        y = xc * lax.rsqrt(var + EPS) * gsgu_ref[:, sl]
        vn_ref[:, sl] = y.astype(vn_ref.dtype)

    head_rms(section(sec_q), q_ref, gq_ref[...], q_scale)
    head_rms(section(sec_k), k_ref, gk_ref[...], None)
    u_ref[...] = jax.nn.gelu(section(sec_u)).astype(u_ref.dtype)
    v_ref[...] = section(sec_v).astype(v_ref.dtype)


def _inproj(x2, mod3, g_mix, w5, w_f, g_q, g_k, g_sgu, *, seq, tm=512):
    M, D = x2.shape
    n_sec = 5
    width = w5.shape[1] // n_sec
    tiles_per_seq = seq // tm
    row = lambda m: (m, 0)
    const2 = lambda m: (0, 0)
    resident = pl.Buffered(1)
    act = jax.ShapeDtypeStruct((M, width), _BF16)
    return pl.pallas_call(
        functools.partial(_inproj_kernel, q_scale=np.float32(_LOG2E / np.sqrt(HEAD_DIM))),
        grid=(M // tm,),
        in_specs=[pl.BlockSpec((tm, D), row),
                  pl.BlockSpec((1, N_MOD, D), lambda m: (m // tiles_per_seq, 0, 0)),
                  pl.BlockSpec((1, D), const2),
                  pl.BlockSpec((D, n_sec * width), const2, pipeline_mode=resident),
                  pl.BlockSpec((D, LANES), const2, pipeline_mode=resident),
                  pl.BlockSpec((1, HEAD_DIM), const2),
                  pl.BlockSpec((1, HEAD_DIM), const2),
                  pl.BlockSpec((1, width), const2)],
        out_specs=[pl.BlockSpec((tm, width), row)] * 5 + [pl.BlockSpec((tm, LANES), row)],
        out_shape=[act] * 5 + [jax.ShapeDtypeStruct((M, LANES), _F32)],
        scratch_shapes=[pltpu.VMEM((tm, D), _BF16)],
        compiler_params=_params(("parallel",), 56),
        name="inproj",
    )(x2, mod3, g_mix, w5, w_f, g_q, g_k, g_sgu)


def _fcumsum_kernel(f_ref, bf_ref, fcol_ref, frow_ref, *, n_heads):
    S = f_ref.shape[0]
    r = lax.broadcasted_iota(jnp.int32, (CHUNK, CHUNK), 0)
    cidx = lax.broadcasted_iota(jnp.int32, (CHUNK, CHUNK), 1)
    tri = jnp.where(cidx <= r, 1.0, 0.0).astype(_BF16)
    carry = jnp.zeros((1, LANES), _F32)
    for ci in range(S // CHUNK):
        sl = slice(ci * CHUNK, (ci + 1) * CHUNK)
        lf = jax.nn.log_sigmoid(f_ref[sl, :] + bf_ref[...])
        hi = lf.astype(_BF16)
        r1 = lf - hi.astype(_F32)
        mid = r1.astype(_BF16)
        lo = (r1 - mid.astype(_F32)).astype(_BF16)
        cs = (jnp.dot(tri, hi, preferred_element_type=_F32)
              + jnp.dot(tri, mid, preferred_element_type=_F32)
              + jnp.dot(tri, lo, preferred_element_type=_F32))
        F = cs + carry
        F2 = F * _LOG2E
        fcol_ref[sl, :] = F2
        frow_ref[0, :, sl] = F2.T[0:n_heads, :]
        carry = F[CHUNK - 1:CHUNK, :]


def _fcumsum(flog, b_f_pad, *, batch, seq, n_heads):
    return pl.pallas_call(
        functools.partial(_fcumsum_kernel, n_heads=n_heads),
        grid=(batch,),
        in_specs=[pl.BlockSpec((seq, LANES), lambda b: (b, 0)),
                  pl.BlockSpec((1, LANES), lambda b: (0, 0))],
        out_specs=[pl.BlockSpec((seq, LANES), lambda b: (b, 0)),
                   pl.BlockSpec((1, n_heads, seq), lambda b: (b, 0, 0))],
        out_shape=[jax.ShapeDtypeStruct((batch * seq, LANES), _F32),
                   jax.ShapeDtypeStruct((batch, n_heads, seq), _F32)],
        compiler_params=_params(("parallel",), 32),
        name="fcumsum",
    )(flog, b_f_pad)


def _attn_kernel(q_ref, k_ref, v_ref, fcol_ref, frow_ref, w1_ref, w2_ref,
                 o_ref, w1b_ref, w2b_ref, v2_scr, s_scr, *, tq, hps):
    w1b_ref[...] = w1_ref[...].astype(w1b_ref.dtype)
    w2b_ref[...] = w2_ref[...].astype(w2b_ref.dtype)

    hp = pl.program_id(1)
    S = q_ref.shape[1]
    n_blk = S // tq
    hsl = [slice(hh * HEAD_DIM, (hh + 1) * HEAD_DIM) for hh in range(hps)]
    den = slice(HEAD_DIM, 2 * HEAD_DIM)
    tiles = [(i, j) for i in range(n_blk) for j in range(i + 1)]

    for hh in range(hps):
        v2_scr[hh, :, 0:HEAD_DIM] = v_ref[0, :, hsl[hh]]
        v2_scr[hh, :, den] = jnp.ones((S, HEAD_DIM), _BF16)

    lane = lax.broadcasted_iota(jnp.int32, (tq, LANES), 1)
    rows = lax.broadcasted_iota(jnp.int32, (tq, tq), 0)
    cols = lax.broadcasted_iota(jnp.int32, (tq, tq), 1)
    causal = cols <= rows

    def scores(t):
        i, j = tiles[t]
        for hh in range(hps):
            s = lax.dot_general(q_ref[0, i * tq:(i + 1) * tq, hsl[hh]],
                                k_ref[0, j * tq:(j + 1) * tq, hsl[hh]],
                                (((1,), (1,)), ((), ())), preferred_element_type=_F32)
            s_scr[t % 2, hh] = s - frow_ref[0, 0, hh:hh + 1, j * tq:(j + 1) * tq]

    scores(0)
    fq = m = acc = None
    for t, (i, j) in enumerate(tiles):
        if t + 1 < len(tiles):
            scores(t + 1)
        if j == 0:
            fc = fcol_ref[0, i * tq:(i + 1) * tq, :]
            fq = [jnp.sum(jnp.where(lane == hp * hps + hh, fc, 0.0), axis=-1, keepdims=True)
                  for hh in range(hps)]
            m = [None] * hps
            acc = [None] * hps
        for hh in range(hps):
            a = s_scr[t % 2, hh]
            if j == i:
                a = jnp.where(causal, a, -jnp.inf)
            m_tile = jnp.max(a, axis=-1, keepdims=True) + fq[hh]
            m_new = m_tile if j == 0 else jnp.maximum(m[hh], m_tile)
            p = jnp.exp2(a - (m_new - fq[hh])).astype(_BF16)
            pv = jnp.dot(p, v2_scr[hh, j * tq:(j + 1) * tq, :], preferred_element_type=_F32)
            acc[hh] = pv if j == 0 else jnp.exp2(m[hh] - m_new) * acc[hh] + pv
            m[hh] = m_new
            if j == i:
                o_ref[0, i * tq:(i + 1) * tq, hsl[hh]] = (
                    acc[hh][:, 0:HEAD_DIM] / acc[hh][:, den]).astype(o_ref.dtype)


def _attention(q, k, v, fcol, frow, w1, w2, *, tq=512, hps=2):
    B, S, W = q.shape
    H = W // HEAD_DIM
    n_hp = H // hps
    n_steps = B * n_hp
    r1, r2 = w1.shape[0] // n_steps, w2.shape[0] // n_steps
    assert r1 * n_steps == w1.shape[0] and r2 * n_steps == w2.shape[0]
    heads = lambda b, h: (b, 0, h)
    slab = lambda b, h: (b * n_hp + h, 0)
    return pl.pallas_call(
        functools.partial(_attn_kernel, tq=tq, hps=hps),
        grid=(B, n_hp),
        in_specs=[pl.BlockSpec((1, S, hps * HEAD_DIM), heads),
                  pl.BlockSpec((1, S, hps * HEAD_DIM), heads),
                  pl.BlockSpec((1, S, hps * HEAD_DIM), heads),
                  pl.BlockSpec((1, S, LANES), lambda b, h: (b, 0, 0)),
                  pl.BlockSpec((1, 1, hps, S), lambda b, h: (b, h, 0, 0)),
                  pl.BlockSpec((r1, w1.shape[1]), slab),
                  pl.BlockSpec((r2, w2.shape[1]), slab)],
        out_specs=[pl.BlockSpec((1, S, hps * HEAD_DIM), heads),
                   pl.BlockSpec((r1, w1.shape[1]), slab),
                   pl.BlockSpec((r2, w2.shape[1]), slab)],
        out_shape=[jax.ShapeDtypeStruct((B, S, W), _BF16),
                   jax.ShapeDtypeStruct(w1.shape, _BF16),
                   jax.ShapeDtypeStruct(w2.shape, _BF16)],
        scratch_shapes=[pltpu.VMEM((hps, S, 2 * HEAD_DIM), _BF16),
                        pltpu.VMEM((2, hps, tq, tq), _F32)],
        compiler_params=_params(("parallel", "parallel"), 48),
        name="attention",
    )(q, k, v, fcol, frow.reshape(B, n_hp, hps, S), w1, w2)


def _sgu_kernel(u_ref, vn_ref, ws_ref, bs_ref, o_ref):
    tm, width = u_ref.shape
    n_groups = width // HEAD_DIM
    r = lax.broadcasted_iota(jnp.int32, (CHUNK, CHUNK), 0)
    cidx = lax.broadcasted_iota(jnp.int32, (CHUNK, CHUNK), 1)
    lower = cidx <= r
    for g in range(n_groups):
        w = jnp.where(lower, ws_ref[g], 0.0).astype(_BF16)
        bcol = jnp.broadcast_to(bs_ref[g:g + 1, :], (CHUNK, CHUNK)).T
        cs = slice(g * HEAD_DIM, (g + 1) * HEAD_DIM)
        for ci in range(tm // CHUNK):
            rs = slice(ci * CHUNK, (ci + 1) * CHUNK)
            z = jnp.dot(w, vn_ref[rs, cs], preferred_element_type=_F32) + bcol
            o_ref[rs, cs] = (u_ref[rs, cs].astype(_F32) * z).astype(o_ref.dtype)


def _sgu(u, vn, w_s, b_s, *, tm=512):
    M, width = u.shape
    G = w_s.shape[0]
    row = lambda m: (m, 0)
    return pl.pallas_call(
        _sgu_kernel,
        grid=(M // tm,),
        in_specs=[pl.BlockSpec((tm, width), row),
                  pl.BlockSpec((tm, width), row),
                  pl.BlockSpec((G, CHUNK, CHUNK), lambda m: (0, 0, 0)),
                  pl.BlockSpec((G, CHUNK), lambda m: (0, 0))],
        out_specs=pl.BlockSpec((tm, width), row),
        out_shape=jax.ShapeDtypeStruct((M, width), _BF16),
        compiler_params=_params(("parallel",), 32),
        name="sgu",
    )(u, vn, w_s, b_s)


def _outproj_kernel(a_ref, s_ref, w_ref, x_ref, mod_ref, o_ref):
    wa = a_ref.shape[1]
    mix = (jnp.dot(a_ref[...], w_ref[0:wa, :], preferred_element_type=_F32)
           + jnp.dot(s_ref[...], w_ref[wa:, :], preferred_element_type=_F32))
    o_ref[...] = x_ref[...] + mod_ref[0, _GATE1:_GATE1 + 1, :] * mix


def _outproj(attn, sgu, w_out, x2, mod3, *, seq, tm=512):
    M, D = x2.shape
    wa, ws = attn.shape[1], sgu.shape[1]
    tiles_per_seq = seq // tm
    row = lambda m: (m, 0)
    return pl.pallas_call(
        _outproj_kernel,
        grid=(M // tm,),
        in_specs=[pl.BlockSpec((tm, wa), row),
                  pl.BlockSpec((tm, ws), row),
                  pl.BlockSpec((wa + ws, D), lambda m: (0, 0)),
                  pl.BlockSpec((tm, D), row),
                  pl.BlockSpec((1, N_MOD, D), lambda m: (m // tiles_per_seq, 0, 0))],
        out_specs=pl.BlockSpec((tm, D), row),
        out_shape=jax.ShapeDtypeStruct((M, D), _F32),
        compiler_params=_params(("parallel",), 48),
        name="outproj",
    )(attn, sgu, w_out, x2, mod3)


def _ffn_kernel(x_ref, mod_ref, g_ref, w1_ref, w2_ref, o_ref, h_scr):
    f = pl.program_id(1)

    @pl.when(f == 0)
    def _():
        x = x_ref[...]
        h = _norm_modulate(x, g_ref[...],
                           mod_ref[0, _SHIFT2:_SHIFT2 + 1, :], mod_ref[0, _SCALE2:_SCALE2 + 1, :])
        h_scr[...] = h.astype(_BF16)
        o_ref[...] = x

    hid = jnp.dot(h_scr[...], w1_ref[...], preferred_element_type=_F32)
    hid = jnp.square(jnp.maximum(hid, 0.0)).astype(_BF16)
    part = jnp.dot(hid, w2_ref[...], preferred_element_type=_F32)
    o_ref[...] += mod_ref[0, _GATE2:_GATE2 + 1, :] * part


def _ffn(x2, mod3, g_ffn, w1, w2, *, seq, tm=512, tf=1024):
    M, D = x2.shape
    F = w1.shape[1]
    tiles_per_seq = seq // tm
    row = lambda m, f: (m, 0)
    return pl.pallas_call(
        _ffn_kernel,
        grid=(M // tm, F // tf),
        in_specs=[pl.BlockSpec((tm, D), row),
                  pl.BlockSpec((1, N_MOD, D), lambda m, f: (m // tiles_per_seq, 0, 0)),
                  pl.BlockSpec((1, D), lambda m, f: (0, 0)),
                  pl.BlockSpec((D, tf), lambda m, f: (0, f)),
                  pl.BlockSpec((tf, D), lambda m, f: (f, 0))],
        out_specs=pl.BlockSpec((tm, D), row),
        out_shape=jax.ShapeDtypeStruct((M, D), _F32),
        scratch_shapes=[pltpu.VMEM((tm, D), _BF16)],
        compiler_params=_params(("parallel", "arbitrary"), 52),
        name="ffn",
    )(x2, mod3, g_ffn, w1, w2)


def kernel(x, c, w_ada, b_ada, g_mix, w_in, b_f, g_q, g_k, g_sgu, w_s, b_s, w_out, g_ffn,
           w_ff1, w_ff2):
    B, S, D = x.shape
    depth = w_ada.shape[0]
    H = b_f.shape[1]
    G = w_s.shape[1]
    attn_w = H * HEAD_DIM
    gmlp_w = G * HEAD_DIM
    x2 = x.reshape(B * S, D)
    for l in range(depth):
        mod3 = _adaln(c, w_ada[l], b_ada[l]).reshape(B, N_MOD, D)

        wl = w_in[l]
        o_f = 3 * attn_w
        o_u = o_f + H
        w5 = jnp.concatenate([wl[:, 0:o_f], wl[:, o_u:]], axis=1).astype(_BF16)
        w_f = jnp.pad(wl[:, o_f:o_u], ((0, 0), (0, LANES - H))).astype(_BF16)
        b_f_pad = jnp.pad(b_f[l], (0, LANES - H)).reshape(1, LANES)

        q, k, v, u, vn, flog = _inproj(
            x2, mod3, g_mix[l].reshape(1, D), w5, w_f, g_q[l].reshape(1, HEAD_DIM),
            g_k[l].reshape(1, HEAD_DIM), g_sgu[l].reshape(1, gmlp_w), seq=S)

        fcol, frow = _fcumsum(flog, b_f_pad, batch=B, seq=S, n_heads=H)
        attn, w1b, w2b = _attention(q.reshape(B, S, attn_w), k.reshape(B, S, attn_w),
                                    v.reshape(B, S, attn_w), fcol.reshape(B, S, LANES), frow,
                                    w_ff1[l], w_ff2[l])
        sgu = _sgu(u, vn, w_s[l], b_s[l])
        x2 = _outproj(attn.reshape(B * S, attn_w), sgu, w_out[l].astype(_BF16), x2, mod3, seq=S)
        x2 = _ffn(x2, mod3, g_ffn[l].reshape(1, D), w1b, w2b, seq=S)
    return x2.reshape(B, S, D)
```

```python
import functools

import jax
import jax.numpy as jnp
import numpy as np
from jax import lax
from jax.experimental import pallas as pl
from jax.experimental.pallas import tpu as pltpu

_F32 = jnp.float32
_BF16 = jnp.bfloat16

HEAD_DIM = 128
CHUNK = 128
N_MOD = 6
EPS = 1e-6
LANES = 128
_LOG2E = np.float32(np.log2(np.e))

_SHIFT1, _SCALE1, _GATE1, _SHIFT2, _SCALE2, _GATE2 = range(6)

_MIB = 1024 * 1024


def _params(semantics, vmem_mib):
    return pltpu.CompilerParams(dimension_semantics=semantics,
                                vmem_limit_bytes=vmem_mib * _MIB)


def _adaln_kernel(c_ref, w_ref, b_ref, win_ref, o_ref, w5_ref, wf_ref, *, o_f, o_u):
    c = c_ref[...]
    c_act = (c * jax.nn.sigmoid(c)).astype(_BF16)
    w = w_ref[...].astype(_BF16)
    o_ref[...] = jnp.dot(c_act, w, preferred_element_type=_F32) + b_ref[...]

    n_tail = w5_ref.shape[1] - o_f
    w5_ref[:, 0:o_f] = win_ref[:, 0:o_f].astype(w5_ref.dtype)
    w5_ref[:, o_f:] = win_ref[:, o_u:o_u + n_tail].astype(w5_ref.dtype)
    lane = lax.broadcasted_iota(jnp.int32, wf_ref.shape, 1)
    wf_ref[...] = jnp.where(lane < o_u - o_f, win_ref[:, o_f:o_f + LANES], 0.0).astype(wf_ref.dtype)


def _adaln(c, w_ada, b_ada, w_in, *, o_f, o_u, n_steps=16):
    B, D = c.shape
    N = w_ada.shape[1]
    cols = w_in.shape[1]
    tn, rows = N // n_steps, D // n_steps
    assert tn * n_steps == N and tn % LANES == 0 and rows * n_steps == D
    n_keep = cols - (o_u - o_f)
    slab = lambda n: (n, 0)
    return pl.pallas_call(
        functools.partial(_adaln_kernel, o_f=o_f, o_u=o_u),
        grid=(n_steps,),
        in_specs=[pl.BlockSpec((B, D), lambda n: (0, 0)),
                  pl.BlockSpec((D, tn), lambda n: (0, n)),
                  pl.BlockSpec((1, tn), lambda n: (0, n)),
                  pl.BlockSpec((rows, cols), slab)],
        out_specs=[pl.BlockSpec((B, tn), lambda n: (0, n)),
                   pl.BlockSpec((rows, n_keep), slab),
                   pl.BlockSpec((rows, LANES), slab)],
        out_shape=[jax.ShapeDtypeStruct((B, N), _F32),
                   jax.ShapeDtypeStruct((D, n_keep), _BF16),
                   jax.ShapeDtypeStruct((D, LANES), _BF16)],
        compiler_params=_params(("parallel",), 40),
        name="adaln",
    )(c, w_ada, b_ada.reshape(1, N), w_in)


def _norm_modulate(x, g, shift, scale):
    ms = jnp.mean(x * x, axis=-1, keepdims=True)
    y = x * lax.rsqrt(ms + EPS) * g
    return y * (1.0 + scale) + shift


def _inproj_kernel(x_ref, mod_ref, gmix_ref, w_ref, wf_ref, gq_ref, gk_ref, gsgu_ref,
                   q_ref, k_ref, v_ref, u_ref, vn_ref, f_ref, h_scr, *, q_scale):
    width = q_ref.shape[-1]
    n_heads = width // HEAD_DIM
    sec_q, sec_k, sec_v, sec_u, sec_vg = range(5)

    h = _norm_modulate(x_ref[...], gmix_ref[...],
                       mod_ref[0, _SHIFT1:_SHIFT1 + 1, :], mod_ref[0, _SCALE1:_SCALE1 + 1, :])
    h_scr[...] = h.astype(_BF16)
    f_ref[...] = jnp.dot(h_scr[...], wf_ref[...], preferred_element_type=_F32)

    def section(idx):
        return jnp.dot(h_scr[...], w_ref[:, idx * width:(idx + 1) * width],
                       preferred_element_type=_F32)

    def head_rms(acc, o_ref, g, scale):
        for hh in range(n_heads):
            sl = slice(hh * HEAD_DIM, (hh + 1) * HEAD_DIM)
            blk = acc[:, sl]
            ms = jnp.mean(blk * blk, axis=-1, keepdims=True)
            y = blk * lax.rsqrt(ms + EPS) * g
            if scale is not None:
                y = y * scale
            o_ref[:, sl] = y.astype(o_ref.dtype)

    acc = section(sec_vg)
    for gg in range(n_heads):
        sl = slice(gg * HEAD_DIM, (gg + 1) * HEAD_DIM)
        blk = jax.nn.gelu(acc[:, sl])
        mu = jnp.mean(blk, axis=-1, keepdims=True)
        xc = blk - mu
        var = jnp.mean(xc * xc, axis=-1, keepdims=True)
        y = xc * lax.rsqrt(var + EPS) * gsgu_ref[:, sl]
        vn_ref[:, sl] = y.astype(vn_ref.dtype)

    head_rms(section(sec_q), q_ref, gq_ref[...], q_scale)
    head_rms(section(sec_k), k_ref, gk_ref[...], None)
    u_ref[...] = jax.nn.gelu(section(sec_u)).astype(u_ref.dtype)
    v_ref[...] = section(sec_v).astype(v_ref.dtype)


def _inproj(x2, mod3, g_mix, w5, w_f, g_q, g_k, g_sgu, *, seq, tm=512):
    M, D = x2.shape
    n_sec = 5
    width = w5.shape[1] // n_sec
    tiles_per_seq = seq // tm
    row = lambda m: (m, 0)
    const2 = lambda m: (0, 0)
    resident = pl.Buffered(1)
    act = jax.ShapeDtypeStruct((M, width), _BF16)
    return pl.pallas_call(
        functools.partial(_inproj_kernel, q_scale=np.float32(_LOG2E / np.sqrt(HEAD_DIM))),
        grid=(M // tm,),
        in_specs=[pl.BlockSpec((tm, D), row),
                  pl.BlockSpec((1, N_MOD, D), lambda m: (m // tiles_per_seq, 0, 0)),
                  pl.BlockSpec((1, D), const2),
                  pl.BlockSpec((D, n_sec * width), const2, pipeline_mode=resident),
                  pl.BlockSpec((D, LANES), const2, pipeline_mode=resident),
                  pl.BlockSpec((1, HEAD_DIM), const2),
                  pl.BlockSpec((1, HEAD_DIM), const2),
                  pl.BlockSpec((1, width), const2)],
        out_specs=[pl.BlockSpec((tm, width), row)] * 5 + [pl.BlockSpec((tm, LANES), row)],
        out_shape=[act] * 5 + [jax.ShapeDtypeStruct((M, LANES), _F32)],
        scratch_shapes=[pltpu.VMEM((tm, D), _BF16)],
        compiler_params=_params(("parallel",), 56),
        name="inproj",
    )(x2, mod3, g_mix, w5, w_f, g_q, g_k, g_sgu)


def _fcumsum_kernel(f_ref, bf_ref, fcol_ref, frow_ref, *, n_heads):
    S = f_ref.shape[0]
    r = lax.broadcasted_iota(jnp.int32, (CHUNK, CHUNK), 0)
    cidx = lax.broadcasted_iota(jnp.int32, (CHUNK, CHUNK), 1)
    tri = jnp.where(cidx <= r, 1.0, 0.0).astype(_BF16)
    carry = jnp.zeros((1, LANES), _F32)
    for ci in range(S // CHUNK):
        sl = slice(ci * CHUNK, (ci + 1) * CHUNK)
        lf = jax.nn.log_sigmoid(f_ref[sl, :] + bf_ref[...])
        hi = lf.astype(_BF16)
        r1 = lf - hi.astype(_F32)
        mid = r1.astype(_BF16)
        lo = (r1 - mid.astype(_F32)).astype(_BF16)
        cs = (jnp.dot(tri, hi, preferred_element_type=_F32)
              + jnp.dot(tri, mid, preferred_element_type=_F32)
              + jnp.dot(tri, lo, preferred_element_type=_F32))
        F = cs + carry
        F2 = F * _LOG2E
        fcol_ref[sl, :] = F2
        frow_ref[0, :, sl] = F2.T[0:n_heads, :]
        carry = F[CHUNK - 1:CHUNK, :]


def _fcumsum(flog, b_f_pad, *, batch, seq, n_heads):
    return pl.pallas_call(
        functools.partial(_fcumsum_kernel, n_heads=n_heads),
        grid=(batch,),
        in_specs=[pl.BlockSpec((seq, LANES), lambda b: (b, 0)),
                  pl.BlockSpec((1, LANES), lambda b: (0, 0))],
        out_specs=[pl.BlockSpec((seq, LANES), lambda b: (b, 0)),
                   pl.BlockSpec((1, n_heads, seq), lambda b: (b, 0, 0))],
        out_shape=[jax.ShapeDtypeStruct((batch * seq, LANES), _F32),
                   jax.ShapeDtypeStruct((batch, n_heads, seq), _F32)],
        compiler_params=_params(("parallel",), 32),
        name="fcumsum",
    )(flog, b_f_pad)


def _attn_kernel(q_ref, k_ref, v_ref, fcol_ref, frow_ref, w1_ref, w2_ref,
                 o_ref, w1b_ref, w2b_ref, v2_scr, s_scr, *, tq, hps):
    w1b_ref[...] = w1_ref[...].astype(w1b_ref.dtype)
    w2b_ref[...] = w2_ref[...].astype(w2b_ref.dtype)

    hp = pl.program_id(1)
    S = q_ref.shape[1]
    n_blk = S // tq
    hsl = [slice(hh * HEAD_DIM, (hh + 1) * HEAD_DIM) for hh in range(hps)]
    den = slice(HEAD_DIM, 2 * HEAD_DIM)
    tiles = [(i, j) for i in range(n_blk) for j in range(i + 1)]

    for hh in range(hps):
        v2_scr[hh, :, 0:HEAD_DIM] = v_ref[0, :, hsl[hh]]
        v2_scr[hh, :, den] = jnp.ones((S, HEAD_DIM), _BF16)

    lane = lax.broadcasted_iota(jnp.int32, (tq, LANES), 1)
    rows = lax.broadcasted_iota(jnp.int32, (tq, tq), 0)
    cols = lax.broadcasted_iota(jnp.int32, (tq, tq), 1)
    causal = cols <= rows

    def scores(t):
        i, j = tiles[t]
        for hh in range(hps):
            s = lax.dot_general(q_ref[0, i * tq:(i + 1) * tq, hsl[hh]],
                                k_ref[0, j * tq:(j + 1) * tq, hsl[hh]],
                                (((1,), (1,)), ((), ())), preferred_element_type=_F32)
            s_scr[t % 2, hh] = s - frow_ref[0, 0, hh:hh + 1, j * tq:(j + 1) * tq]

    scores(0)
    fq = m = acc = None
    for t, (i, j) in enumerate(tiles):
        if t + 1 < len(tiles):
            scores(t + 1)
        if j == 0:
            fc = fcol_ref[0, i * tq:(i + 1) * tq, :]
            fq = [jnp.sum(jnp.where(lane == hp * hps + hh, fc, 0.0), axis=-1, keepdims=True)
                  for hh in range(hps)]
            m = [None] * hps
            acc = [None] * hps
        for hh in range(hps):
            a = s_scr[t % 2, hh]
            if j == i:
                a = jnp.where(causal, a, -jnp.inf)
            m_tile = jnp.max(a, axis=-1, keepdims=True) + fq[hh]
            m_new = m_tile if j == 0 else jnp.maximum(m[hh], m_tile)
            p = jnp.exp2(a - (m_new - fq[hh])).astype(_BF16)
            pv = jnp.dot(p, v2_scr[hh, j * tq:(j + 1) * tq, :], preferred_element_type=_F32)
            acc[hh] = pv if j == 0 else jnp.exp2(m[hh] - m_new) * acc[hh] + pv
            m[hh] = m_new
            if j == i:
                o_ref[0, i * tq:(i + 1) * tq, hsl[hh]] = (
                    acc[hh][:, 0:HEAD_DIM] / acc[hh][:, den]).astype(o_ref.dtype)


def _attention(q, k, v, fcol, frow, w1, w2, *, tq=512, hps=2):
    B, S, W = q.shape
    H = W // HEAD_DIM
    n_hp = H // hps
    n_steps = B * n_hp
    r1, r2 = w1.shape[0] // n_steps, w2.shape[0] // n_steps
    assert r1 * n_steps == w1.shape[0] and r2 * n_steps == w2.shape[0]
    heads = lambda b, h: (b, 0, h)
    slab = lambda b, h: (b * n_hp + h, 0)
    return pl.pallas_call(
        functools.partial(_attn_kernel, tq=tq, hps=hps),
        grid=(B, n_hp),
        in_specs=[pl.BlockSpec((1, S, hps * HEAD_DIM), heads),
                  pl.BlockSpec((1, S, hps * HEAD_DIM), heads),
                  pl.BlockSpec((1, S, hps * HEAD_DIM), heads),
                  pl.BlockSpec((1, S, LANES), lambda b, h: (b, 0, 0)),
                  pl.BlockSpec((1, 1, hps, S), lambda b, h: (b, h, 0, 0)),
                  pl.BlockSpec((r1, w1.shape[1]), slab),
                  pl.BlockSpec((r2, w2.shape[1]), slab)],
        out_specs=[pl.BlockSpec((1, S, hps * HEAD_DIM), heads),
                   pl.BlockSpec((r1, w1.shape[1]), slab),
                   pl.BlockSpec((r2, w2.shape[1]), slab)],
        out_shape=[jax.ShapeDtypeStruct((B, S, W), _BF16),
                   jax.ShapeDtypeStruct(w1.shape, _BF16),
                   jax.ShapeDtypeStruct(w2.shape, _BF16)],
        scratch_shapes=[pltpu.VMEM((hps, S, 2 * HEAD_DIM), _BF16),
                        pltpu.VMEM((2, hps, tq, tq), _F32)],
        compiler_params=_params(("parallel", "parallel"), 48),
        name="attention",
    )(q, k, v, fcol, frow.reshape(B, n_hp, hps, S), w1, w2)


def _sgu_kernel(u_ref, vn_ref, ws_ref, bs_ref, o_ref):
    tm, width = u_ref.shape
    n_groups = width // HEAD_DIM
    r = lax.broadcasted_iota(jnp.int32, (CHUNK, CHUNK), 0)
    cidx = lax.broadcasted_iota(jnp.int32, (CHUNK, CHUNK), 1)
    lower = cidx <= r
    for g in range(n_groups):
        w = jnp.where(lower, ws_ref[g], 0.0).astype(_BF16)
        bcol = jnp.broadcast_to(bs_ref[g:g + 1, :], (CHUNK, CHUNK)).T
        cs = slice(g * HEAD_DIM, (g + 1) * HEAD_DIM)
        for ci in range(tm // CHUNK):
            rs = slice(ci * CHUNK, (ci + 1) * CHUNK)
            z = jnp.dot(w, vn_ref[rs, cs], preferred_element_type=_F32) + bcol
            o_ref[rs, cs] = (u_ref[rs, cs].astype(_F32) * z).astype(o_ref.dtype)


def _sgu(u, vn, w_s, b_s, *, tm=512):
    M, width = u.shape
    G = w_s.shape[0]
    row = lambda m: (m, 0)
    return pl.pallas_call(
        _sgu_kernel,
        grid=(M // tm,),
        in_specs=[pl.BlockSpec((tm, width), row),
                  pl.BlockSpec((tm, width), row),
                  pl.BlockSpec((G, CHUNK, CHUNK), lambda m: (0, 0, 0)),
                  pl.BlockSpec((G, CHUNK), lambda m: (0, 0))],
        out_specs=pl.BlockSpec((tm, width), row),
        out_shape=jax.ShapeDtypeStruct((M, width), _BF16),
        compiler_params=_params(("parallel",), 32),
        name="sgu",
    )(u, vn, w_s, b_s)


def _outproj_kernel(a_ref, s_ref, w_ref, x_ref, mod_ref, o_ref):
    wa = a_ref.shape[1]
    mix = (jnp.dot(a_ref[...], w_ref[0:wa, :], preferred_element_type=_F32)
           + jnp.dot(s_ref[...], w_ref[wa:, :], preferred_element_type=_F32))
    o_ref[...] = x_ref[...] + mod_ref[0, _GATE1:_GATE1 + 1, :] * mix


def _outproj(attn, sgu, w_out, x2, mod3, *, seq, tm=512):
    M, D = x2.shape
    wa, ws = attn.shape[1], sgu.shape[1]
    tiles_per_seq = seq // tm
    row = lambda m: (m, 0)
    return pl.pallas_call(
        _outproj_kernel,
        grid=(M // tm,),
        in_specs=[pl.BlockSpec((tm, wa), row),
                  pl.BlockSpec((tm, ws), row),
                  pl.BlockSpec((wa + ws, D), lambda m: (0, 0)),
                  pl.BlockSpec((tm, D), row),
                  pl.BlockSpec((1, N_MOD, D), lambda m: (m // tiles_per_seq, 0, 0))],
        out_specs=pl.BlockSpec((tm, D), row),
        out_shape=jax.ShapeDtypeStruct((M, D), _F32),
        compiler_params=_params(("parallel",), 48),
        name="outproj",
    )(attn, sgu, w_out, x2, mod3)


def _ffn_kernel(x_ref, mod_ref, g_ref, w1_ref, w2_ref, o_ref, h_scr):
    f = pl.program_id(1)

    @pl.when(f == 0)
    def _():
        x = x_ref[...]
        h = _norm_modulate(x, g_ref[...],
                           mod_ref[0, _SHIFT2:_SHIFT2 + 1, :], mod_ref[0, _SCALE2:_SCALE2 + 1, :])
        h_scr[...] = h.astype(_BF16)
        o_ref[...] = x

    hid = jnp.dot(h_scr[...], w1_ref[...], preferred_element_type=_F32)
    hid = jnp.square(jnp.maximum(hid, 0.0)).astype(_BF16)
    part = jnp.dot(hid, w2_ref[...], preferred_element_type=_F32)
    o_ref[...] += mod_ref[0, _GATE2:_GATE2 + 1, :] * part


def _ffn(x2, mod3, g_ffn, w1, w2, *, seq, tm=512, tf=1024):
    M, D = x2.shape
    F = w1.shape[1]
    tiles_per_seq = seq // tm
    row = lambda m, f: (m, 0)
    return pl.pallas_call(
        _ffn_kernel,
        grid=(M // tm, F // tf),
        in_specs=[pl.BlockSpec((tm, D), row),
                  pl.BlockSpec((1, N_MOD, D), lambda m, f: (m // tiles_per_seq, 0, 0)),
                  pl.BlockSpec((1, D), lambda m, f: (0, 0)),
                  pl.BlockSpec((D, tf), lambda m, f: (0, f)),
                  pl.BlockSpec((tf, D), lambda m, f: (f, 0))],
        out_specs=pl.BlockSpec((tm, D), row),
        out_shape=jax.ShapeDtypeStruct((M, D), _F32),
        scratch_shapes=[pltpu.VMEM((tm, D), _BF16)],
        compiler_params=_params(("parallel", "arbitrary"), 52),
        name="ffn",
    )(x2, mod3, g_ffn, w1, w2)


def kernel(x, c, w_ada, b_ada, g_mix, w_in, b_f, g_q, g_k, g_sgu, w_s, b_s, w_out, g_ffn,
           w_ff1, w_ff2):
    B, S, D = x.shape
    depth = w_ada.shape[0]
    H = b_f.shape[1]
    G = w_s.shape[1]
    attn_w = H * HEAD_DIM
    gmlp_w = G * HEAD_DIM
    x2 = x.reshape(B * S, D)
    for l in range(depth):
        o_f = 3 * attn_w
        o_u = o_f + H
        assert attn_w == gmlp_w and w_in.shape[2] == o_u + 2 * gmlp_w
        mod, w5, w_f = _adaln(c, w_ada[l], b_ada[l], w_in[l], o_f=o_f, o_u=o_u)
        mod3 = mod.reshape(B, N_MOD, D)
        b_f_pad = jnp.pad(b_f[l], (0, LANES - H)).reshape(1, LANES)

        q, k, v, u, vn, flog = _inproj(
            x2, mod3, g_mix[l].reshape(1, D), w5, w_f, g_q[l].reshape(1, HEAD_DIM),
            g_k[l].reshape(1, HEAD_DIM), g_sgu[l].reshape(1, gmlp_w), seq=S)

        fcol, frow = _fcumsum(flog, b_f_pad, batch=B, seq=S, n_heads=H)
        attn, w1b, w2b = _attention(q.reshape(B, S, attn_w), k.reshape(B, S, attn_w),
                                    v.reshape(B, S, attn_w), fcol.reshape(B, S, LANES), frow,
                                    w_ff1[l], w_ff2[l])
        sgu = _sgu(u, vn, w_s[l], b_s[l])
        x2 = _outproj(attn.reshape(B * S, attn_w), sgu, w_out[l].astype(_BF16), x2, mod3, seq=S)
        x2 = _ffn(x2, mod3, g_ffn[l].reshape(1, D), w1b, w2b, seq=S)
    return x2.reshape(B, S, D)
```

```python
import functools

import jax
import jax.numpy as jnp
import numpy as np
from jax import lax
from jax.experimental import pallas as pl
from jax.experimental.pallas import tpu as pltpu

_F32 = jnp.float32
_BF16 = jnp.bfloat16

HEAD_DIM = 128
CHUNK = 128
N_MOD = 6
EPS = 1e-6
LANES = 128
_LOG2E = np.float32(np.log2(np.e))

_SHIFT1, _SCALE1, _GATE1, _SHIFT2, _SCALE2, _GATE2 = range(6)

_MIB = 1024 * 1024


def _params(semantics, vmem_mib):
    return pltpu.CompilerParams(dimension_semantics=semantics,
                                vmem_limit_bytes=vmem_mib * _MIB)


def _adaln_kernel(c_ref, w_ref, b_ref, win_ref, o_ref, w5_ref, wf_ref, *, o_f, o_u):
    c = c_ref[...]
    c_act = (c * jax.nn.sigmoid(c)).astype(_BF16)
    w = w_ref[...].astype(_BF16)
    o_ref[...] = jnp.dot(c_act, w, preferred_element_type=_F32) + b_ref[...]

    w5_ref[0:o_f, :] = win_ref[0:o_f, :].astype(w5_ref.dtype)
    w5_ref[o_f:, :] = win_ref[o_u:, :].astype(w5_ref.dtype)
    row = lax.broadcasted_iota(jnp.int32, wf_ref.shape, 0)
    wf_ref[...] = jnp.where(row < o_u - o_f, win_ref[o_f:o_f + LANES, :], 0.0).astype(wf_ref.dtype)


def _adaln(c, w_ada, b_ada, w_in_t, *, o_f, o_u, n_steps=16):
    B, D = c.shape
    N = w_ada.shape[1]
    cols = w_in_t.shape[0]
    tn, td = N // n_steps, D // n_steps
    assert tn * n_steps == N and tn % LANES == 0 and td * n_steps == D and td % LANES == 0
    n_keep = cols - (o_u - o_f)
    dcol = lambda n: (0, n)
    return pl.pallas_call(
        functools.partial(_adaln_kernel, o_f=o_f, o_u=o_u),
        grid=(n_steps,),
        in_specs=[pl.BlockSpec((B, D), lambda n: (0, 0)),
                  pl.BlockSpec((D, tn), dcol),
                  pl.BlockSpec((1, tn), dcol),
                  pl.BlockSpec((cols, td), dcol)],
        out_specs=[pl.BlockSpec((B, tn), dcol),
                   pl.BlockSpec((n_keep, td), dcol),
                   pl.BlockSpec((LANES, td), dcol)],
        out_shape=[jax.ShapeDtypeStruct((B, N), _F32),
                   jax.ShapeDtypeStruct((n_keep, D), _BF16),
                   jax.ShapeDtypeStruct((LANES, D), _BF16)],
        compiler_params=_params(("parallel",), 40),
        name="adaln",
    )(c, w_ada, b_ada.reshape(1, N), w_in_t)


def _norm_modulate(x, g, shift, scale):
    ms = jnp.mean(x * x, axis=-1, keepdims=True)
    y = x * lax.rsqrt(ms + EPS) * g
    return y * (1.0 + scale) + shift


def _inproj_kernel(x_ref, mod_ref, gmix_ref, w_ref, wf_ref, gq_ref, gk_ref, gsgu_ref,
                   q_ref, k_ref, v_ref, u_ref, vn_ref, f_ref, h_scr, *, q_scale):
    width = q_ref.shape[-1]
    n_heads = width // HEAD_DIM
    sec_q, sec_k, sec_v, sec_u, sec_vg = range(5)

    h = _norm_modulate(x_ref[...], gmix_ref[...],
                       mod_ref[0, _SHIFT1:_SHIFT1 + 1, :], mod_ref[0, _SCALE1:_SCALE1 + 1, :])
    h_scr[...] = h.astype(_BF16)
    nt = (((1,), (1,)), ((), ()))
    f_ref[...] = lax.dot_general(h_scr[...], wf_ref[...], nt, preferred_element_type=_F32)

    def section(idx):
        return lax.dot_general(h_scr[...], w_ref[idx * width:(idx + 1) * width, :], nt,
                               preferred_element_type=_F32)

    def head_rms(acc, o_ref, g, scale):
        for hh in range(n_heads):
            sl = slice(hh * HEAD_DIM, (hh + 1) * HEAD_DIM)
            blk = acc[:, sl]
            ms = jnp.mean(blk * blk, axis=-1, keepdims=True)
            y = blk * lax.rsqrt(ms + EPS) * g
            if scale is not None:
                y = y * scale
            o_ref[:, sl] = y.astype(o_ref.dtype)

    acc = section(sec_vg)
    for gg in range(n_heads):
        sl = slice(gg * HEAD_DIM, (gg + 1) * HEAD_DIM)
        blk = jax.nn.gelu(acc[:, sl])
        mu = jnp.mean(blk, axis=-1, keepdims=True)
        xc = blk - mu
        var = jnp.mean(xc * xc, axis=-1, keepdims=True)
        y = xc * lax.rsqrt(var + EPS) * gsgu_ref[:, sl]
        vn_ref[:, sl] = y.astype(vn_ref.dtype)

    head_rms(section(sec_q), q_ref, gq_ref[...], q_scale)
    head_rms(section(sec_k), k_ref, gk_ref[...], None)
    u_ref[...] = jax.nn.gelu(section(sec_u)).astype(u_ref.dtype)
    v_ref[...] = section(sec_v).astype(v_ref.dtype)


def _inproj(x2, mod3, g_mix, w5, w_f, g_q, g_k, g_sgu, *, seq, tm=512):
    M, D = x2.shape
    n_sec = 5
    width = w5.shape[0] // n_sec
    tiles_per_seq = seq // tm
    row = lambda m: (m, 0)
    const2 = lambda m: (0, 0)
    resident = pl.Buffered(1)
    act = jax.ShapeDtypeStruct((M, width), _BF16)
    return pl.pallas_call(
        functools.partial(_inproj_kernel, q_scale=np.float32(_LOG2E / np.sqrt(HEAD_DIM))),
        grid=(M // tm,),
        in_specs=[pl.BlockSpec((tm, D), row),
                  pl.BlockSpec((1, N_MOD, D), lambda m: (m // tiles_per_seq, 0, 0)),
                  pl.BlockSpec((1, D), const2),
                  pl.BlockSpec((n_sec * width, D), const2, pipeline_mode=resident),
                  pl.BlockSpec((LANES, D), const2, pipeline_mode=resident),
                  pl.BlockSpec((1, HEAD_DIM), const2),
                  pl.BlockSpec((1, HEAD_DIM), const2),
                  pl.BlockSpec((1, width), const2)],
        out_specs=[pl.BlockSpec((tm, width), row)] * 5 + [pl.BlockSpec((tm, LANES), row)],
        out_shape=[act] * 5 + [jax.ShapeDtypeStruct((M, LANES), _F32)],
        scratch_shapes=[pltpu.VMEM((tm, D), _BF16)],
        compiler_params=_params(("parallel",), 56),
        name="inproj",
    )(x2, mod3, g_mix, w5, w_f, g_q, g_k, g_sgu)


def _fcumsum_kernel(f_ref, bf_ref, fcol_ref, frow_ref, *, n_heads):
    S = f_ref.shape[0]
    r = lax.broadcasted_iota(jnp.int32, (CHUNK, CHUNK), 0)
    cidx = lax.broadcasted_iota(jnp.int32, (CHUNK, CHUNK), 1)
    tri = jnp.where(cidx <= r, 1.0, 0.0).astype(_BF16)
    carry = jnp.zeros((1, LANES), _F32)
    for ci in range(S // CHUNK):
        sl = slice(ci * CHUNK, (ci + 1) * CHUNK)
        lf = jax.nn.log_sigmoid(f_ref[sl, :] + bf_ref[...])
        hi = lf.astype(_BF16)
        r1 = lf - hi.astype(_F32)
        mid = r1.astype(_BF16)
        lo = (r1 - mid.astype(_F32)).astype(_BF16)
        cs = (jnp.dot(tri, hi, preferred_element_type=_F32)
              + jnp.dot(tri, mid, preferred_element_type=_F32)
              + jnp.dot(tri, lo, preferred_element_type=_F32))
        F = cs + carry
        F2 = F * _LOG2E
        fcol_ref[sl, :] = F2
        frow_ref[0, :, sl] = F2.T[0:n_heads, :]
        carry = F[CHUNK - 1:CHUNK, :]


def _fcumsum(flog, b_f_pad, *, batch, seq, n_heads):
    return pl.pallas_call(
        functools.partial(_fcumsum_kernel, n_heads=n_heads),
        grid=(batch,),
        in_specs=[pl.BlockSpec((seq, LANES), lambda b: (b, 0)),
                  pl.BlockSpec((1, LANES), lambda b: (0, 0))],
        out_specs=[pl.BlockSpec((seq, LANES), lambda b: (b, 0)),
                   pl.BlockSpec((1, n_heads, seq), lambda b: (b, 0, 0))],
        out_shape=[jax.ShapeDtypeStruct((batch * seq, LANES), _F32),
                   jax.ShapeDtypeStruct((batch, n_heads, seq), _F32)],
        compiler_params=_params(("parallel",), 32),
        name="fcumsum",
    )(flog, b_f_pad)


def _attn_kernel(q_ref, k_ref, v_ref, fcol_ref, frow_ref, w1_ref, w2_ref,
                 o_ref, w1b_ref, w2b_ref, v2_scr, s_scr, *, tq, hps):
    w1b_ref[...] = w1_ref[...].astype(w1b_ref.dtype)
    w2b_ref[...] = w2_ref[...].astype(w2b_ref.dtype)

    hp = pl.program_id(1)
    S = q_ref.shape[1]
    n_blk = S // tq
    hsl = [slice(hh * HEAD_DIM, (hh + 1) * HEAD_DIM) for hh in range(hps)]
    den = slice(HEAD_DIM, 2 * HEAD_DIM)
    tiles = [(i, j) for i in range(n_blk) for j in range(i + 1)]

    for hh in range(hps):
        v2_scr[hh, :, 0:HEAD_DIM] = v_ref[0, :, hsl[hh]]
        v2_scr[hh, :, den] = jnp.ones((S, HEAD_DIM), _BF16)

    lane = lax.broadcasted_iota(jnp.int32, (tq, LANES), 1)
    rows = lax.broadcasted_iota(jnp.int32, (tq, tq), 0)
    cols = lax.broadcasted_iota(jnp.int32, (tq, tq), 1)
    causal = cols <= rows

    def scores(t):
        i, j = tiles[t]
        for hh in range(hps):
            s = lax.dot_general(q_ref[0, i * tq:(i + 1) * tq, hsl[hh]],
                                k_ref[0, j * tq:(j + 1) * tq, hsl[hh]],
                                (((1,), (1,)), ((), ())), preferred_element_type=_F32)
            s_scr[t % 2, hh] = s - frow_ref[0, 0, hh:hh + 1, j * tq:(j + 1) * tq]

    scores(0)
    fq = m = acc = None
    for t, (i, j) in enumerate(tiles):
        if t + 1 < len(tiles):
            scores(t + 1)
        if j == 0:
            fc = fcol_ref[0, i * tq:(i + 1) * tq, :]
            fq = [jnp.sum(jnp.where(lane == hp * hps + hh, fc, 0.0), axis=-1, keepdims=True)
                  for hh in range(hps)]
            m = [None] * hps
            acc = [None] * hps
        for hh in range(hps):
            a = s_scr[t % 2, hh]
            if j == i:
                a = jnp.where(causal, a, -jnp.inf)
            m_tile = jnp.max(a, axis=-1, keepdims=True) + fq[hh]
            m_new = m_tile if j == 0 else jnp.maximum(m[hh], m_tile)
            p = jnp.exp2(a - (m_new - fq[hh])).astype(_BF16)
            pv = jnp.dot(p, v2_scr[hh, j * tq:(j + 1) * tq, :], preferred_element_type=_F32)
            acc[hh] = pv if j == 0 else jnp.exp2(m[hh] - m_new) * acc[hh] + pv
            m[hh] = m_new
            if j == i:
                o_ref[0, i * tq:(i + 1) * tq, hsl[hh]] = (
                    acc[hh][:, 0:HEAD_DIM] / acc[hh][:, den]).astype(o_ref.dtype)


def _attention(q, k, v, fcol, frow, w1, w2, *, tq=512, hps=2):
    B, S, W = q.shape
    H = W // HEAD_DIM
    n_hp = H // hps
    n_steps = B * n_hp
    r1, r2 = w1.shape[0] // n_steps, w2.shape[0] // n_steps
    assert r1 * n_steps == w1.shape[0] and r2 * n_steps == w2.shape[0]
    heads = lambda b, h: (b, 0, h)
    slab = lambda b, h: (b * n_hp + h, 0)
    return pl.pallas_call(
        functools.partial(_attn_kernel, tq=tq, hps=hps),
        grid=(B, n_hp),
        in_specs=[pl.BlockSpec((1, S, hps * HEAD_DIM), heads),
                  pl.BlockSpec((1, S, hps * HEAD_DIM), heads),
                  pl.BlockSpec((1, S, hps * HEAD_DIM), heads),
                  pl.BlockSpec((1, S, LANES), lambda b, h: (b, 0, 0)),
                  pl.BlockSpec((1, 1, hps, S), lambda b, h: (b, h, 0, 0)),
                  pl.BlockSpec((r1, w1.shape[1]), slab),
                  pl.BlockSpec((r2, w2.shape[1]), slab)],
        out_specs=[pl.BlockSpec((1, S, hps * HEAD_DIM), heads),
                   pl.BlockSpec((r1, w1.shape[1]), slab),
                   pl.BlockSpec((r2, w2.shape[1]), slab)],
        out_shape=[jax.ShapeDtypeStruct((B, S, W), _BF16),
                   jax.ShapeDtypeStruct(w1.shape, _BF16),
                   jax.ShapeDtypeStruct(w2.shape, _BF16)],
        scratch_shapes=[pltpu.VMEM((hps, S, 2 * HEAD_DIM), _BF16),
                        pltpu.VMEM((2, hps, tq, tq), _F32)],
        compiler_params=_params(("parallel", "parallel"), 48),
        name="attention",
    )(q, k, v, fcol, frow.reshape(B, n_hp, hps, S), w1, w2)


def _sgu_kernel(u_ref, vn_ref, ws_ref, bs_ref, o_ref):
    tm, width = u_ref.shape
    n_groups = width // HEAD_DIM
    r = lax.broadcasted_iota(jnp.int32, (CHUNK, CHUNK), 0)
    cidx = lax.broadcasted_iota(jnp.int32, (CHUNK, CHUNK), 1)
    lower = cidx <= r
    for g in range(n_groups):
        w = jnp.where(lower, ws_ref[g], 0.0).astype(_BF16)
        bcol = jnp.broadcast_to(bs_ref[g:g + 1, :], (CHUNK, CHUNK)).T
        cs = slice(g * HEAD_DIM, (g + 1) * HEAD_DIM)
        for ci in range(tm // CHUNK):
            rs = slice(ci * CHUNK, (ci + 1) * CHUNK)
            z = jnp.dot(w, vn_ref[rs, cs], preferred_element_type=_F32) + bcol
            o_ref[rs, cs] = (u_ref[rs, cs].astype(_F32) * z).astype(o_ref.dtype)


def _sgu(u, vn, w_s, b_s, *, tm=512):
    M, width = u.shape
    G = w_s.shape[0]
    row = lambda m: (m, 0)
    return pl.pallas_call(
        _sgu_kernel,
        grid=(M // tm,),
        in_specs=[pl.BlockSpec((tm, width), row),
                  pl.BlockSpec((tm, width), row),
                  pl.BlockSpec((G, CHUNK, CHUNK), lambda m: (0, 0, 0)),
                  pl.BlockSpec((G, CHUNK), lambda m: (0, 0))],
        out_specs=pl.BlockSpec((tm, width), row),
        out_shape=jax.ShapeDtypeStruct((M, width), _BF16),
        compiler_params=_params(("parallel",), 32),
        name="sgu",
    )(u, vn, w_s, b_s)


def _outproj_kernel(a_ref, s_ref, w_ref, x_ref, mod_ref, o_ref):
    wa = a_ref.shape[1]
    mix = (jnp.dot(a_ref[...], w_ref[0:wa, :], preferred_element_type=_F32)
           + jnp.dot(s_ref[...], w_ref[wa:, :], preferred_element_type=_F32))
    o_ref[...] = x_ref[...] + mod_ref[0, _GATE1:_GATE1 + 1, :] * mix


def _outproj(attn, sgu, w_out, x2, mod3, *, seq, tm=512):
    M, D = x2.shape
    wa, ws = attn.shape[1], sgu.shape[1]
    tiles_per_seq = seq // tm
    row = lambda m: (m, 0)
    return pl.pallas_call(
        _outproj_kernel,
        grid=(M // tm,),
        in_specs=[pl.BlockSpec((tm, wa), row),
                  pl.BlockSpec((tm, ws), row),
                  pl.BlockSpec((wa + ws, D), lambda m: (0, 0)),
                  pl.BlockSpec((tm, D), row),
                  pl.BlockSpec((1, N_MOD, D), lambda m: (m // tiles_per_seq, 0, 0))],
        out_specs=pl.BlockSpec((tm, D), row),
        out_shape=jax.ShapeDtypeStruct((M, D), _F32),
        compiler_params=_params(("parallel",), 48),
        name="outproj",
    )(attn, sgu, w_out, x2, mod3)


def _ffn_kernel(x_ref, mod_ref, g_ref, w1_ref, w2_ref, o_ref, h_scr):
    f = pl.program_id(1)

    @pl.when(f == 0)
    def _():
        x = x_ref[...]
        h = _norm_modulate(x, g_ref[...],
                           mod_ref[0, _SHIFT2:_SHIFT2 + 1, :], mod_ref[0, _SCALE2:_SCALE2 + 1, :])
        h_scr[...] = h.astype(_BF16)
        o_ref[...] = x

    hid = jnp.dot(h_scr[...], w1_ref[...], preferred_element_type=_F32)
    hid = jnp.square(jnp.maximum(hid, 0.0)).astype(_BF16)
    part = jnp.dot(hid, w2_ref[...], preferred_element_type=_F32)
    o_ref[...] += mod_ref[0, _GATE2:_GATE2 + 1, :] * part


def _ffn(x2, mod3, g_ffn, w1, w2, *, seq, tm=512, tf=1024):
    M, D = x2.shape
    F = w1.shape[1]
    tiles_per_seq = seq // tm
    row = lambda m, f: (m, 0)
    return pl.pallas_call(
        _ffn_kernel,
        grid=(M // tm, F // tf),
        in_specs=[pl.BlockSpec((tm, D), row),
                  pl.BlockSpec((1, N_MOD, D), lambda m, f: (m // tiles_per_seq, 0, 0)),
                  pl.BlockSpec((1, D), lambda m, f: (0, 0)),
                  pl.BlockSpec((D, tf), lambda m, f: (0, f)),
                  pl.BlockSpec((tf, D), lambda m, f: (f, 0))],
        out_specs=pl.BlockSpec((tm, D), row),
        out_shape=jax.ShapeDtypeStruct((M, D), _F32),
        scratch_shapes=[pltpu.VMEM((tm, D), _BF16)],
        compiler_params=_params(("parallel", "arbitrary"), 52),
        name="ffn",
    )(x2, mod3, g_ffn, w1, w2)


def kernel(x, c, w_ada, b_ada, g_mix, w_in, b_f, g_q, g_k, g_sgu, w_s, b_s, w_out, g_ffn,
           w_ff1, w_ff2):
    B, S, D = x.shape
    depth = w_ada.shape[0]
    H = b_f.shape[1]
    G = w_s.shape[1]
    attn_w = H * HEAD_DIM
    gmlp_w = G * HEAD_DIM
    x2 = x.reshape(B * S, D)
    for l in range(depth):
        o_f = 3 * attn_w
        o_u = o_f + H
        assert attn_w == gmlp_w and w_in.shape[2] == o_u + 2 * gmlp_w
        mod, w5, w_f = _adaln(c, w_ada[l], b_ada[l], w_in[l].T, o_f=o_f, o_u=o_u)
        mod3 = mod.reshape(B, N_MOD, D)
        b_f_pad = jnp.pad(b_f[l], (0, LANES - H)).reshape(1, LANES)

        q, k, v, u, vn, flog = _inproj(
            x2, mod3, g_mix[l].reshape(1, D), w5, w_f, g_q[l].reshape(1, HEAD_DIM),
            g_k[l].reshape(1, HEAD_DIM), g_sgu[l].reshape(1, gmlp_w), seq=S)

        fcol, frow = _fcumsum(flog, b_f_pad, batch=B, seq=S, n_heads=H)
        attn, w1b, w2b = _attention(q.reshape(B, S, attn_w), k.reshape(B, S, attn_w),
                                    v.reshape(B, S, attn_w), fcol.reshape(B, S, LANES), frow,
                                    w_ff1[l], w_ff2[l])
        sgu = _sgu(u, vn, w_s[l], b_s[l])
        x2 = _outproj(attn.reshape(B * S, attn_w), sgu, w_out[l].astype(_BF16), x2, mod3, seq=S)
        x2 = _ffn(x2, mod3, g_ffn[l].reshape(1, D), w1b, w2b, seq=S)
    return x2.reshape(B, S, D)
```

```python
import functools

import jax
import jax.numpy as jnp
import numpy as np
from jax import lax
from jax.experimental import pallas as pl
from jax.experimental.pallas import tpu as pltpu

_F32 = jnp.float32
_BF16 = jnp.bfloat16

HEAD_DIM = 128
CHUNK = 128
N_MOD = 6
EPS = 1e-6
LANES = 128
_LOG2E = np.float32(np.log2(np.e))

_SHIFT1, _SCALE1, _GATE1, _SHIFT2, _SCALE2, _GATE2 = range(6)

_MIB = 1024 * 1024


def _params(semantics, vmem_mib):
    return pltpu.CompilerParams(dimension_semantics=semantics,
                                vmem_limit_bytes=vmem_mib * _MIB)


def _adaln_kernel(c_ref, w_ref, b_ref, win_ref, o_ref, w5_ref, wf_ref, *, o_f, o_u):
    c = c_ref[...]
    c_act = (c * jax.nn.sigmoid(c)).astype(_BF16)
    w = w_ref[...].astype(_BF16)
    o_ref[...] = jnp.dot(c_act, w, preferred_element_type=_F32) + b_ref[...]

    w5_ref[0:o_f, :] = win_ref[0:o_f, :].astype(w5_ref.dtype)
    w5_ref[o_f:, :] = win_ref[o_u:, :].astype(w5_ref.dtype)
    row = lax.broadcasted_iota(jnp.int32, wf_ref.shape, 0)
    wf_ref[...] = jnp.where(row < o_u - o_f, win_ref[o_f:o_f + LANES, :], 0.0).astype(wf_ref.dtype)


def _adaln(c, w_ada, b_ada, w_in_t, *, o_f, o_u, n_steps=16):
    B, D = c.shape
    N = w_ada.shape[1]
    cols = w_in_t.shape[0]
    tn, td = N // n_steps, D // n_steps
    assert tn * n_steps == N and tn % LANES == 0 and td * n_steps == D and td % LANES == 0
    n_keep = cols - (o_u - o_f)
    dcol = lambda n: (0, n)
    return pl.pallas_call(
        functools.partial(_adaln_kernel, o_f=o_f, o_u=o_u),
        grid=(n_steps,),
        in_specs=[pl.BlockSpec((B, D), lambda n: (0, 0)),
                  pl.BlockSpec((D, tn), dcol),
                  pl.BlockSpec((1, tn), dcol),
                  pl.BlockSpec((cols, td), dcol)],
        out_specs=[pl.BlockSpec((B, tn), dcol),
                   pl.BlockSpec((n_keep, td), dcol),
                   pl.BlockSpec((LANES, td), dcol)],
        out_shape=[jax.ShapeDtypeStruct((B, N), _F32),
                   jax.ShapeDtypeStruct((n_keep, D), _BF16),
                   jax.ShapeDtypeStruct((LANES, D), _BF16)],
        compiler_params=_params(("parallel",), 40),
        name="adaln",
    )(c, w_ada, b_ada.reshape(1, N), w_in_t)


def _norm_modulate(x, g, shift, scale):
    ms = jnp.mean(x * x, axis=-1, keepdims=True)
    y = x * lax.rsqrt(ms + EPS) * g
    return y * (1.0 + scale) + shift


def _inproj_kernel(x_ref, mod_ref, gmix_ref, w_ref, wf_ref, gq_ref, gk_ref, gsgu_ref, ws_ref,
                   bs_ref, q_ref, k_ref, v_ref, sgu_ref, f_ref, h_scr, vn_scr, z_scr, *, q_scale):
    width = q_ref.shape[-1]
    n_heads = width // HEAD_DIM
    sec_q, sec_k, sec_v, sec_u, sec_vg = range(5)

    h = _norm_modulate(x_ref[...], gmix_ref[...],
                       mod_ref[0, _SHIFT1:_SHIFT1 + 1, :], mod_ref[0, _SCALE1:_SCALE1 + 1, :])
    h_scr[...] = h.astype(_BF16)
    nt = (((1,), (1,)), ((), ()))
    f_ref[...] = lax.dot_general(h_scr[...], wf_ref[...], nt, preferred_element_type=_F32)

    def section(idx):
        return lax.dot_general(h_scr[...], w_ref[idx * width:(idx + 1) * width, :], nt,
                               preferred_element_type=_F32)

    def head_rms(acc, o_ref, g, scale):
        for hh in range(n_heads):
            sl = slice(hh * HEAD_DIM, (hh + 1) * HEAD_DIM)
            blk = acc[:, sl]
            ms = jnp.mean(blk * blk, axis=-1, keepdims=True)
            y = blk * lax.rsqrt(ms + EPS) * g
            if scale is not None:
                y = y * scale
            o_ref[:, sl] = y.astype(o_ref.dtype)

    acc = section(sec_vg)
    for gg in range(n_heads):
        sl = slice(gg * HEAD_DIM, (gg + 1) * HEAD_DIM)
        blk = jax.nn.gelu(acc[:, sl])
        mu = jnp.mean(blk, axis=-1, keepdims=True)
        xc = blk - mu
        var = jnp.mean(xc * xc, axis=-1, keepdims=True)
        y = xc * lax.rsqrt(var + EPS) * gsgu_ref[:, sl]
        vn_scr[:, sl] = y.astype(vn_scr.dtype)

    r = lax.broadcasted_iota(jnp.int32, (CHUNK, CHUNK), 0)
    cidx = lax.broadcasted_iota(jnp.int32, (CHUNK, CHUNK), 1)
    lower = cidx <= r
    for gg in range(n_heads):
        sl = slice(gg * HEAD_DIM, (gg + 1) * HEAD_DIM)
        w = jnp.where(lower, ws_ref[gg], 0.0).astype(_BF16)
        bcol = jnp.broadcast_to(bs_ref[gg:gg + 1, :], (CHUNK, CHUNK)).T
        for ci in range(x_ref.shape[0] // CHUNK):
            rs = slice(ci * CHUNK, (ci + 1) * CHUNK)
            z_scr[rs, sl] = jnp.dot(w, vn_scr[rs, sl], preferred_element_type=_F32) + bcol

    head_rms(section(sec_q), q_ref, gq_ref[...], q_scale)
    head_rms(section(sec_k), k_ref, gk_ref[...], None)
    sgu_ref[...] = (jax.nn.gelu(section(sec_u)) * z_scr[...]).astype(sgu_ref.dtype)
    v_ref[...] = section(sec_v).astype(v_ref.dtype)


def _inproj(x2, mod3, g_mix, w5, w_f, g_q, g_k, g_sgu, w_s, b_s, *, seq, tm=512):
    M, D = x2.shape
    n_sec = 5
    width = w5.shape[0] // n_sec
    tiles_per_seq = seq // tm
    row = lambda m: (m, 0)
    const2 = lambda m: (0, 0)
    resident = pl.Buffered(1)
    act = jax.ShapeDtypeStruct((M, width), _BF16)
    return pl.pallas_call(
        functools.partial(_inproj_kernel, q_scale=np.float32(_LOG2E / np.sqrt(HEAD_DIM))),
        grid=(M // tm,),
        in_specs=[pl.BlockSpec((tm, D), row),
                  pl.BlockSpec((1, N_MOD, D), lambda m: (m // tiles_per_seq, 0, 0)),
                  pl.BlockSpec((1, D), const2),
                  pl.BlockSpec((n_sec * width, D), const2, pipeline_mode=resident),
                  pl.BlockSpec((LANES, D), const2, pipeline_mode=resident),
                  pl.BlockSpec((1, HEAD_DIM), const2),
                  pl.BlockSpec((1, HEAD_DIM), const2),
                  pl.BlockSpec((1, width), const2),
                  pl.BlockSpec(w_s.shape, lambda m: (0, 0, 0)),
                  pl.BlockSpec(b_s.shape, const2)],
        out_specs=[pl.BlockSpec((tm, width), row)] * 4 + [pl.BlockSpec((tm, LANES), row)],
        out_shape=[act] * 4 + [jax.ShapeDtypeStruct((M, LANES), _F32)],
        scratch_shapes=[pltpu.VMEM((tm, D), _BF16), pltpu.VMEM((tm, width), _BF16),
                        pltpu.VMEM((tm, width), _F32)],
        compiler_params=_params(("parallel",), 56),
        name="inproj",
    )(x2, mod3, g_mix, w5, w_f, g_q, g_k, g_sgu, w_s, b_s)


def _fcumsum_kernel(f_ref, bf_ref, fcol_ref, frow_ref, *, n_heads):
    S = f_ref.shape[0]
    r = lax.broadcasted_iota(jnp.int32, (CHUNK, CHUNK), 0)
    cidx = lax.broadcasted_iota(jnp.int32, (CHUNK, CHUNK), 1)
    tri = jnp.where(cidx <= r, 1.0, 0.0).astype(_BF16)
    carry = jnp.zeros((1, LANES), _F32)
    for ci in range(S // CHUNK):
        sl = slice(ci * CHUNK, (ci + 1) * CHUNK)
        lf = jax.nn.log_sigmoid(f_ref[sl, :] + bf_ref[...])
        hi = lf.astype(_BF16)
        r1 = lf - hi.astype(_F32)
        mid = r1.astype(_BF16)
        lo = (r1 - mid.astype(_F32)).astype(_BF16)
        cs = (jnp.dot(tri, hi, preferred_element_type=_F32)
              + jnp.dot(tri, mid, preferred_element_type=_F32)
              + jnp.dot(tri, lo, preferred_element_type=_F32))
        F = cs + carry
        F2 = F * _LOG2E
        fcol_ref[sl, :] = F2
        frow_ref[0, :, sl] = F2.T[0:n_heads, :]
        carry = F[CHUNK - 1:CHUNK, :]


def _fcumsum(flog, b_f_pad, *, batch, seq, n_heads):
    return pl.pallas_call(
        functools.partial(_fcumsum_kernel, n_heads=n_heads),
        grid=(batch,),
        in_specs=[pl.BlockSpec((seq, LANES), lambda b: (b, 0)),
                  pl.BlockSpec((1, LANES), lambda b: (0, 0))],
        out_specs=[pl.BlockSpec((seq, LANES), lambda b: (b, 0)),
                   pl.BlockSpec((1, n_heads, seq), lambda b: (b, 0, 0))],
        out_shape=[jax.ShapeDtypeStruct((batch * seq, LANES), _F32),
                   jax.ShapeDtypeStruct((batch, n_heads, seq), _F32)],
        compiler_params=_params(("parallel",), 32),
        name="fcumsum",
    )(flog, b_f_pad)


def _attn_kernel(q_ref, k_ref, v_ref, fcol_ref, frow_ref, w1_ref, w2_ref, w3_ref,
                 o_ref, w1b_ref, w2b_ref, w3b_ref, v2_scr, s_scr, *, tq, hps):
    w1b_ref[...] = w1_ref[...].astype(w1b_ref.dtype)
    w2b_ref[...] = w2_ref[...].astype(w2b_ref.dtype)
    w3b_ref[...] = w3_ref[...].astype(w3b_ref.dtype)

    hp = pl.program_id(1)
    S = q_ref.shape[1]
    n_blk = S // tq
    hsl = [slice(hh * HEAD_DIM, (hh + 1) * HEAD_DIM) for hh in range(hps)]
    den = slice(HEAD_DIM, 2 * HEAD_DIM)
    tiles = [(i, j) for i in range(n_blk) for j in range(i + 1)]

    for hh in range(hps):
        v2_scr[hh, :, 0:HEAD_DIM] = v_ref[0, :, hsl[hh]]
        v2_scr[hh, :, den] = jnp.ones((S, HEAD_DIM), _BF16)

    lane = lax.broadcasted_iota(jnp.int32, (tq, LANES), 1)
    rows = lax.broadcasted_iota(jnp.int32, (tq, tq), 0)
    cols = lax.broadcasted_iota(jnp.int32, (tq, tq), 1)
    causal = cols <= rows

    def scores(t):
        i, j = tiles[t]
        for hh in range(hps):
            s = lax.dot_general(q_ref[0, i * tq:(i + 1) * tq, hsl[hh]],
                                k_ref[0, j * tq:(j + 1) * tq, hsl[hh]],
                                (((1,), (1,)), ((), ())), preferred_element_type=_F32)
            s_scr[t % 2, hh] = s - frow_ref[0, 0, hh:hh + 1, j * tq:(j + 1) * tq]

    scores(0)
    fq = m = acc = None
    for t, (i, j) in enumerate(tiles):
        if t + 1 < len(tiles):
            scores(t + 1)
        if j == 0:
            fc = fcol_ref[0, i * tq:(i + 1) * tq, :]
            fq = [jnp.sum(jnp.where(lane == hp * hps + hh, fc, 0.0), axis=-1, keepdims=True)
                  for hh in range(hps)]
            m = [None] * hps
            acc = [None] * hps
        for hh in range(hps):
            a = s_scr[t % 2, hh]
            if j == i:
                a = jnp.where(causal, a, -jnp.inf)
            m_tile = jnp.max(a, axis=-1, keepdims=True) + fq[hh]
            m_new = m_tile if j == 0 else jnp.maximum(m[hh], m_tile)
            p = jnp.exp2(a - (m_new - fq[hh])).astype(_BF16)
            pv = jnp.dot(p, v2_scr[hh, j * tq:(j + 1) * tq, :], preferred_element_type=_F32)
            acc[hh] = pv if j == 0 else jnp.exp2(m[hh] - m_new) * acc[hh] + pv
            m[hh] = m_new
            if j == i:
                o_ref[0, i * tq:(i + 1) * tq, hsl[hh]] = (
                    acc[hh][:, 0:HEAD_DIM] / acc[hh][:, den]).astype(o_ref.dtype)


def _attention(q, k, v, fcol, frow, weights, *, tq=512, hps=2):
    B, S, W = q.shape
    H = W // HEAD_DIM
    n_hp = H // hps
    n_steps = B * n_hp
    heads = lambda b, h: (b, 0, h)
    slab = lambda b, h: (b * n_hp + h, 0)
    w_specs = []
    for w in weights:
        rows = w.shape[0] // n_steps
        assert rows * n_steps == w.shape[0]
        w_specs.append(pl.BlockSpec((rows, w.shape[1]), slab))
    return pl.pallas_call(
        functools.partial(_attn_kernel, tq=tq, hps=hps),
        grid=(B, n_hp),
        in_specs=[pl.BlockSpec((1, S, hps * HEAD_DIM), heads),
                  pl.BlockSpec((1, S, hps * HEAD_DIM), heads),
                  pl.BlockSpec((1, S, hps * HEAD_DIM), heads),
                  pl.BlockSpec((1, S, LANES), lambda b, h: (b, 0, 0)),
                  pl.BlockSpec((1, 1, hps, S), lambda b, h: (b, h, 0, 0))] + w_specs,
        out_specs=[pl.BlockSpec((1, S, hps * HEAD_DIM), heads)] + w_specs,
        out_shape=[jax.ShapeDtypeStruct((B, S, W), _BF16)]
                  + [jax.ShapeDtypeStruct(w.shape, _BF16) for w in weights],
        scratch_shapes=[pltpu.VMEM((hps, S, 2 * HEAD_DIM), _BF16),
                        pltpu.VMEM((2, hps, tq, tq), _F32)],
        compiler_params=_params(("parallel", "parallel"), 48),
        name="attention",
    )(q, k, v, fcol, frow.reshape(B, n_hp, hps, S), *weights)


def _outproj_kernel(a_ref, s_ref, w_ref, x_ref, mod_ref, o_ref):
    wa = a_ref.shape[1]
    mix = (jnp.dot(a_ref[...], w_ref[0:wa, :], preferred_element_type=_F32)
           + jnp.dot(s_ref[...], w_ref[wa:, :], preferred_element_type=_F32))
    o_ref[...] = x_ref[...] + mod_ref[0, _GATE1:_GATE1 + 1, :] * mix


def _outproj(attn, sgu, w_out, x2, mod3, *, seq, tm=512):
    M, D = x2.shape
    wa, ws = attn.shape[1], sgu.shape[1]
    tiles_per_seq = seq // tm
    row = lambda m: (m, 0)
    return pl.pallas_call(
        _outproj_kernel,
        grid=(M // tm,),
        in_specs=[pl.BlockSpec((tm, wa), row),
                  pl.BlockSpec((tm, ws), row),
                  pl.BlockSpec((wa + ws, D), lambda m: (0, 0)),
                  pl.BlockSpec((tm, D), row),
                  pl.BlockSpec((1, N_MOD, D), lambda m: (m // tiles_per_seq, 0, 0))],
        out_specs=pl.BlockSpec((tm, D), row),
        out_shape=jax.ShapeDtypeStruct((M, D), _F32),
        compiler_params=_params(("parallel",), 48),
        name="outproj",
    )(attn, sgu, w_out, x2, mod3)


def _ffn_kernel(x_ref, mod_ref, g_ref, w1_ref, w2_ref, o_ref, h_scr):
    f = pl.program_id(1)

    @pl.when(f == 0)
    def _():
        x = x_ref[...]
        h = _norm_modulate(x, g_ref[...],
                           mod_ref[0, _SHIFT2:_SHIFT2 + 1, :], mod_ref[0, _SCALE2:_SCALE2 + 1, :])
        h_scr[...] = h.astype(_BF16)
        o_ref[...] = x

    hid = jnp.dot(h_scr[...], w1_ref[...], preferred_element_type=_F32)
    hid = jnp.square(jnp.maximum(hid, 0.0)).astype(_BF16)
    part = jnp.dot(hid, w2_ref[...], preferred_element_type=_F32)
    o_ref[...] += mod_ref[0, _GATE2:_GATE2 + 1, :] * part


def _ffn(x2, mod3, g_ffn, w1, w2, *, seq, tm=512, tf=1024):
    M, D = x2.shape
    F = w1.shape[1]
    tiles_per_seq = seq // tm
    row = lambda m, f: (m, 0)
    return pl.pallas_call(
        _ffn_kernel,
        grid=(M // tm, F // tf),
        in_specs=[pl.BlockSpec((tm, D), row),
                  pl.BlockSpec((1, N_MOD, D), lambda m, f: (m // tiles_per_seq, 0, 0)),
                  pl.BlockSpec((1, D), lambda m, f: (0, 0)),
                  pl.BlockSpec((D, tf), lambda m, f: (0, f)),
                  pl.BlockSpec((tf, D), lambda m, f: (f, 0))],
        out_specs=pl.BlockSpec((tm, D), row),
        out_shape=jax.ShapeDtypeStruct((M, D), _F32),
        scratch_shapes=[pltpu.VMEM((tm, D), _BF16)],
        compiler_params=_params(("parallel", "arbitrary"), 52),
        name="ffn",
    )(x2, mod3, g_ffn, w1, w2)


def kernel(x, c, w_ada, b_ada, g_mix, w_in, b_f, g_q, g_k, g_sgu, w_s, b_s, w_out, g_ffn,
           w_ff1, w_ff2):
    B, S, D = x.shape
    depth = w_ada.shape[0]
    H = b_f.shape[1]
    G = w_s.shape[1]
    attn_w = H * HEAD_DIM
    gmlp_w = G * HEAD_DIM
    x2 = x.reshape(B * S, D)
    for l in range(depth):
        o_f = 3 * attn_w
        o_u = o_f + H
        assert attn_w == gmlp_w and w_in.shape[2] == o_u + 2 * gmlp_w
        mod, w5, w_f = _adaln(c, w_ada[l], b_ada[l], w_in[l].T, o_f=o_f, o_u=o_u)
        mod3 = mod.reshape(B, N_MOD, D)
        b_f_pad = jnp.pad(b_f[l], (0, LANES - H)).reshape(1, LANES)

        q, k, v, sgu, flog = _inproj(
            x2, mod3, g_mix[l].reshape(1, D), w5, w_f, g_q[l].reshape(1, HEAD_DIM),
            g_k[l].reshape(1, HEAD_DIM), g_sgu[l].reshape(1, gmlp_w), w_s[l], b_s[l], seq=S)

        fcol, frow = _fcumsum(flog, b_f_pad, batch=B, seq=S, n_heads=H)
        attn, w1b, w2b, wob = _attention(
            q.reshape(B, S, attn_w), k.reshape(B, S, attn_w), v.reshape(B, S, attn_w),
            fcol.reshape(B, S, LANES), frow, (w_ff1[l], w_ff2[l], w_out[l]))
        x2 = _outproj(attn.reshape(B * S, attn_w), sgu, wob, x2, mod3, seq=S)
        x2 = _ffn(x2, mod3, g_ffn[l].reshape(1, D), w1b, w2b, seq=S)
    return x2.reshape(B, S, D)
```

```python
import functools

import jax
import jax.numpy as jnp
import numpy as np
from jax import lax
from jax.experimental import pallas as pl
from jax.experimental.pallas import tpu as pltpu

_F32 = jnp.float32
_BF16 = jnp.bfloat16

HEAD_DIM = 128
CHUNK = 128
N_MOD = 6
EPS = 1e-6
LANES = 128
_LOG2E = np.float32(np.log2(np.e))

_SHIFT1, _SCALE1, _GATE1, _SHIFT2, _SCALE2, _GATE2 = range(6)

_MIB = 1024 * 1024


def _params(semantics, vmem_mib):
    return pltpu.CompilerParams(dimension_semantics=semantics,
                                vmem_limit_bytes=vmem_mib * _MIB)


def _adaln_kernel(c_ref, w_ref, b_ref, win_ref, o_ref, w5_ref, wf_ref, *, o_f, o_u):
    c = c_ref[...]
    c_act = (c * jax.nn.sigmoid(c)).astype(_BF16)
    w = w_ref[...].astype(_BF16)
    o_ref[...] = jnp.dot(c_act, w, preferred_element_type=_F32) + b_ref[...]

    w5_ref[0:o_f, :] = win_ref[0:o_f, :].astype(w5_ref.dtype)
    w5_ref[o_f:, :] = win_ref[o_u:, :].astype(w5_ref.dtype)
    row = lax.broadcasted_iota(jnp.int32, wf_ref.shape, 0)
    wf_ref[...] = jnp.where(row < o_u - o_f, win_ref[o_f:o_f + LANES, :], 0.0).astype(wf_ref.dtype)


def _adaln(c, w_ada, b_ada, w_in_t, *, o_f, o_u, n_steps=16):
    B, D = c.shape
    N = w_ada.shape[1]
    cols = w_in_t.shape[0]
    tn, td = N // n_steps, D // n_steps
    assert tn * n_steps == N and tn % LANES == 0 and td * n_steps == D and td % LANES == 0
    n_keep = cols - (o_u - o_f)
    dcol = lambda n: (0, n)
    return pl.pallas_call(
        functools.partial(_adaln_kernel, o_f=o_f, o_u=o_u),
        grid=(n_steps,),
        in_specs=[pl.BlockSpec((B, D), lambda n: (0, 0)),
                  pl.BlockSpec((D, tn), dcol),
                  pl.BlockSpec((1, tn), dcol),
                  pl.BlockSpec((cols, td), dcol)],
        out_specs=[pl.BlockSpec((B, tn), dcol),
                   pl.BlockSpec((n_keep, td), dcol),
                   pl.BlockSpec((LANES, td), dcol)],
        out_shape=[jax.ShapeDtypeStruct((B, N), _F32),
                   jax.ShapeDtypeStruct((n_keep, D), _BF16),
                   jax.ShapeDtypeStruct((LANES, D), _BF16)],
        compiler_params=_params(("parallel",), 40),
        name="adaln",
    )(c, w_ada, b_ada.reshape(1, N), w_in_t)


_NORM_ROWS = 16


def _norm_modulate(x_ref, g, shift, scale, h_ref, copy_ref=None):
    one_plus_scale = 1.0 + scale
    for r in range(0, x_ref.shape[0], _NORM_ROWS):
        rows = slice(r, r + _NORM_ROWS)
        x = x_ref[rows, :]
        ms = jnp.mean(x * x, axis=-1, keepdims=True)
        y = x * lax.rsqrt(ms + EPS) * g
        h_ref[rows, :] = (y * one_plus_scale + shift).astype(h_ref.dtype)
        if copy_ref is not None:
            copy_ref[rows, :] = x


def _inproj_kernel(x_ref, mod_ref, gmix_ref, w_ref, wf_ref, gq_ref, gk_ref, gsgu_ref, ws_ref,
                   bs_ref, q_ref, k_ref, v_ref, sgu_ref, f_ref, h_scr, vn_scr, z_scr, *, q_scale):
    width = q_ref.shape[-1]
    n_heads = width // HEAD_DIM
    sec_q, sec_k, sec_v, sec_u, sec_vg = range(5)

    _norm_modulate(x_ref, gmix_ref[...], mod_ref[0, _SHIFT1:_SHIFT1 + 1, :],
                   mod_ref[0, _SCALE1:_SCALE1 + 1, :], h_scr)
    nt =(((1,), (1,)), ((), ()))
    f_ref[...] = lax.dot_general(h_scr[...], wf_ref[...], nt, preferred_element_type=_F32)

    def section(idx):
        return lax.dot_general(h_scr[...], w_ref[idx * width:(idx + 1) * width, :], nt,
                               preferred_element_type=_F32)

    def head_rms(acc, o_ref, g, scale):
        for hh in range(n_heads):
            sl = slice(hh * HEAD_DIM, (hh + 1) * HEAD_DIM)
            blk = acc[:, sl]
            ms = jnp.mean(blk * blk, axis=-1, keepdims=True)
            y = blk * lax.rsqrt(ms + EPS) * g
            if scale is not None:
                y = y * scale
            o_ref[:, sl] = y.astype(o_ref.dtype)

    acc = section(sec_vg)
    for gg in range(n_heads):
        sl = slice(gg * HEAD_DIM, (gg + 1) * HEAD_DIM)
        blk = jax.nn.gelu(acc[:, sl])
        mu = jnp.mean(blk, axis=-1, keepdims=True)
        xc = blk - mu
        var = jnp.mean(xc * xc, axis=-1, keepdims=True)
        y = xc * lax.rsqrt(var + EPS) * gsgu_ref[:, sl]
        vn_scr[:, sl] = y.astype(vn_scr.dtype)

    r = lax.broadcasted_iota(jnp.int32, (CHUNK, CHUNK), 0)
    cidx = lax.broadcasted_iota(jnp.int32, (CHUNK, CHUNK), 1)
    lower = cidx <= r
    for gg in range(n_heads):
        sl = slice(gg * HEAD_DIM, (gg + 1) * HEAD_DIM)
        w = jnp.where(lower, ws_ref[gg], 0.0).astype(_BF16)
        bcol = jnp.broadcast_to(bs_ref[gg:gg + 1, :], (CHUNK, CHUNK)).T
        for ci in range(x_ref.shape[0] // CHUNK):
            rs = slice(ci * CHUNK, (ci + 1) * CHUNK)
            z_scr[rs, sl] = jnp.dot(w, vn_scr[rs, sl], preferred_element_type=_F32) + bcol

    head_rms(section(sec_q), q_ref, gq_ref[...], q_scale)
    head_rms(section(sec_k), k_ref, gk_ref[...], None)
    sgu_ref[...] = (jax.nn.gelu(section(sec_u)) * z_scr[...]).astype(sgu_ref.dtype)
    v_ref[...] = section(sec_v).astype(v_ref.dtype)


def _inproj(x2, mod3, g_mix, w5, w_f, g_q, g_k, g_sgu, w_s, b_s, *, seq, tm=512):
    M, D = x2.shape
    n_sec = 5
    width = w5.shape[0] // n_sec
    tiles_per_seq = seq // tm
    row = lambda m: (m, 0)
    const2 = lambda m: (0, 0)
    resident = pl.Buffered(1)
    act = jax.ShapeDtypeStruct((M, width), _BF16)
    return pl.pallas_call(
        functools.partial(_inproj_kernel, q_scale=np.float32(_LOG2E / np.sqrt(HEAD_DIM))),
        grid=(M // tm,),
        in_specs=[pl.BlockSpec((tm, D), row),
                  pl.BlockSpec((1, N_MOD, D), lambda m: (m // tiles_per_seq, 0, 0)),
                  pl.BlockSpec((1, D), const2),
                  pl.BlockSpec((n_sec * width, D), const2, pipeline_mode=resident),
                  pl.BlockSpec((LANES, D), const2, pipeline_mode=resident),
                  pl.BlockSpec((1, HEAD_DIM), const2),
                  pl.BlockSpec((1, HEAD_DIM), const2),
                  pl.BlockSpec((1, width), const2),
                  pl.BlockSpec(w_s.shape, lambda m: (0, 0, 0)),
                  pl.BlockSpec(b_s.shape, const2)],
        out_specs=[pl.BlockSpec((tm, width), row)] * 4 + [pl.BlockSpec((tm, LANES), row)],
        out_shape=[act] * 4 + [jax.ShapeDtypeStruct((M, LANES), _F32)],
        scratch_shapes=[pltpu.VMEM((tm, D), _BF16), pltpu.VMEM((tm, width), _BF16),
                        pltpu.VMEM((tm, width), _F32)],
        compiler_params=_params(("parallel",), 56),
        name="inproj",
    )(x2, mod3, g_mix, w5, w_f, g_q, g_k, g_sgu, w_s, b_s)


def _fcumsum_kernel(f_ref, bf_ref, fcol_ref, frow_ref, *, n_heads):
    S = f_ref.shape[0]
    r = lax.broadcasted_iota(jnp.int32, (CHUNK, CHUNK), 0)
    cidx = lax.broadcasted_iota(jnp.int32, (CHUNK, CHUNK), 1)
    tri = jnp.where(cidx <= r, 1.0, 0.0).astype(_BF16)
    carry = jnp.zeros((1, LANES), _F32)
    for ci in range(S // CHUNK):
        sl = slice(ci * CHUNK, (ci + 1) * CHUNK)
        lf = jax.nn.log_sigmoid(f_ref[sl, :] + bf_ref[...])
        hi = lf.astype(_BF16)
        r1 = lf - hi.astype(_F32)
        mid = r1.astype(_BF16)
        lo = (r1 - mid.astype(_F32)).astype(_BF16)
        cs = (jnp.dot(tri, hi, preferred_element_type=_F32)
              + jnp.dot(tri, mid, preferred_element_type=_F32)
              + jnp.dot(tri, lo, preferred_element_type=_F32))
        F = cs + carry
        F2 = F * _LOG2E
        fcol_ref[sl, :] = F2
        frow_ref[0, :, sl] = F2.T[0:n_heads, :]
        carry = F[CHUNK - 1:CHUNK, :]


def _fcumsum(flog, b_f_pad, *, batch, seq, n_heads):
    return pl.pallas_call(
        functools.partial(_fcumsum_kernel, n_heads=n_heads),
        grid=(batch,),
        in_specs=[pl.BlockSpec((seq, LANES), lambda b: (b, 0)),
                  pl.BlockSpec((1, LANES), lambda b: (0, 0))],
        out_specs=[pl.BlockSpec((seq, LANES), lambda b: (b, 0)),
                   pl.BlockSpec((1, n_heads, seq), lambda b: (b, 0, 0))],
        out_shape=[jax.ShapeDtypeStruct((batch * seq, LANES), _F32),
                   jax.ShapeDtypeStruct((batch, n_heads, seq), _F32)],
        compiler_params=_params(("parallel",), 32),
        name="fcumsum",
    )(flog, b_f_pad)


def _attn_kernel(q_ref, k_ref, v_ref, fcol_ref, frow_ref, w1_ref, w2_ref, w3_ref,
                 o_ref, w1b_ref, w2b_ref, w3b_ref, v2_scr, s_scr, *, tq, hps):
    w1b_ref[...] = w1_ref[...].astype(w1b_ref.dtype)
    w2b_ref[...] = w2_ref[...].astype(w2b_ref.dtype)
    w3b_ref[...] = w3_ref[...].astype(w3b_ref.dtype)

    hp = pl.program_id(1)
    S = q_ref.shape[1]
    n_blk = S // tq
    hsl = [slice(hh * HEAD_DIM, (hh + 1) * HEAD_DIM) for hh in range(hps)]
    den = slice(HEAD_DIM, 2 * HEAD_DIM)
    tiles = [(i, j) for i in range(n_blk) for j in range(i + 1)]

    for hh in range(hps):
        v2_scr[hh, :, 0:HEAD_DIM] = v_ref[0, :, hsl[hh]]
        v2_scr[hh, :, den] = jnp.ones((S, HEAD_DIM), _BF16)

    lane = lax.broadcasted_iota(jnp.int32, (tq, LANES), 1)
    rows = lax.broadcasted_iota(jnp.int32, (tq, tq), 0)
    cols = lax.broadcasted_iota(jnp.int32, (tq, tq), 1)
    causal = cols <= rows

    half = tq // 2

    def parts(i, j):
        if j < i:
            return [(0, tq, 0, tq)]
        return [(0, half, 0, half), (half, tq, 0, tq)]

    def scores(t):
        i, j = tiles[t]
        for hh in range(hps):
            for r0, r1, c0, c1 in parts(i, j):
                s = lax.dot_general(q_ref[0, i * tq + r0:i * tq + r1, hsl[hh]],
                                    k_ref[0, j * tq + c0:j * tq + c1, hsl[hh]],
                                    (((1,), (1,)), ((), ())), preferred_element_type=_F32)
                s_scr[t % 2, hh, r0:r1, c0:c1] = (
                    s - frow_ref[0, 0, hh:hh + 1, j * tq + c0:j * tq + c1])

    scores(0)
    fq = m = acc = None
    for t, (i, j) in enumerate(tiles):
        if t + 1 < len(tiles):
            scores(t + 1)
        if j == 0:
            fc = fcol_ref[0, i * tq:(i + 1) * tq, :]
            fq = [jnp.sum(jnp.where(lane == hp * hps + hh, fc, 0.0), axis=-1, keepdims=True)
                  for hh in range(hps)]
            m = [None] * hps
            acc = [None] * hps
        for hh in range(hps):
            for r0, r1, c0, c1 in parts(i, j):
                a = s_scr[t % 2, hh, r0:r1, c0:c1]
                if j == i:
                    a = jnp.where(causal[r0:r1, c0:c1], a, -jnp.inf)
                fq_r = fq[hh][r0:r1, :]
                m_tile = jnp.max(a, axis=-1, keepdims=True) + fq_r
                m_new = m_tile if j == 0 else jnp.maximum(m[hh][r0:r1, :], m_tile)
                p = jnp.exp2(a - (m_new - fq_r)).astype(_BF16)
                pv = jnp.dot(p, v2_scr[hh, j * tq + c0:j * tq + c1, :],
                             preferred_element_type=_F32)
                acc_new = pv if j == 0 else (
                    jnp.exp2(m[hh][r0:r1, :] - m_new) * acc[hh][r0:r1, :] + pv)
                if j == i:
                    o_ref[0, i * tq + r0:i * tq + r1, hsl[hh]] = (
                        acc_new[:, 0:HEAD_DIM] / acc_new[:, den]).astype(o_ref.dtype)
                else:
                    m[hh], acc[hh] = m_new, acc_new


def _attention(q, k, v, fcol, frow, weights, *, tq=512, hps=2):
    B, S, W = q.shape
    H = W // HEAD_DIM
    n_hp = H // hps
    n_steps = B * n_hp
    heads = lambda b, h: (b, 0, h)
    slab = lambda b, h: (b * n_hp + h, 0)
    w_specs = []
    for w in weights:
        rows = w.shape[0] // n_steps
        assert rows * n_steps == w.shape[0]
        w_specs.append(pl.BlockSpec((rows, w.shape[1]), slab))
    return pl.pallas_call(
        functools.partial(_attn_kernel, tq=tq, hps=hps),
        grid=(B, n_hp),
        in_specs=[pl.BlockSpec((1, S, hps * HEAD_DIM), heads),
                  pl.BlockSpec((1, S, hps * HEAD_DIM), heads),
                  pl.BlockSpec((1, S, hps * HEAD_DIM), heads),
                  pl.BlockSpec((1, S, LANES), lambda b, h: (b, 0, 0)),
                  pl.BlockSpec((1, 1, hps, S), lambda b, h: (b, h, 0, 0))] + w_specs,
        out_specs=[pl.BlockSpec((1, S, hps * HEAD_DIM), heads)] + w_specs,
        out_shape=[jax.ShapeDtypeStruct((B, S, W), _BF16)]
                  + [jax.ShapeDtypeStruct(w.shape, _BF16) for w in weights],
        scratch_shapes=[pltpu.VMEM((hps, S, 2 * HEAD_DIM), _BF16),
                        pltpu.VMEM((2, hps, tq, tq), _F32)],
        compiler_params=_params(("parallel", "parallel"), 48),
        name="attention",
    )(q, k, v, fcol, frow.reshape(B, n_hp, hps, S), *weights)


def _outproj_kernel(a_ref, s_ref, w_ref, x_ref, mod_ref, o_ref):
    wa = a_ref.shape[1]
    mix = (jnp.dot(a_ref[...], w_ref[0:wa, :], preferred_element_type=_F32)
           + jnp.dot(s_ref[...], w_ref[wa:, :], preferred_element_type=_F32))
    o_ref[...] = x_ref[...] + mod_ref[0, _GATE1:_GATE1 + 1, :] * mix


def _outproj(attn, sgu, w_out, x2, mod3, *, seq, tm=512):
    M, D = x2.shape
    wa, ws = attn.shape[1], sgu.shape[1]
    tiles_per_seq = seq // tm
    row = lambda m: (m, 0)
    return pl.pallas_call(
        _outproj_kernel,
        grid=(M // tm,),
        in_specs=[pl.BlockSpec((tm, wa), row),
                  pl.BlockSpec((tm, ws), row),
                  pl.BlockSpec((wa + ws, D), lambda m: (0, 0)),
                  pl.BlockSpec((tm, D), row),
                  pl.BlockSpec((1, N_MOD, D), lambda m: (m // tiles_per_seq, 0, 0))],
        out_specs=pl.BlockSpec((tm, D), row),
        out_shape=jax.ShapeDtypeStruct((M, D), _F32),
        compiler_params=_params(("parallel",), 48),
        name="outproj",
    )(attn, sgu, w_out, x2, mod3)


def _ffn_kernel(x_ref, mod_ref, g_ref, w1_ref, w2_ref, o_ref, h_scr):
    f = pl.program_id(1)

    @pl.when(f == 0)
    def _():
        _norm_modulate(x_ref, g_ref[...], mod_ref[0, _SHIFT2:_SHIFT2 + 1, :],
                       mod_ref[0, _SCALE2:_SCALE2 + 1, :], h_scr, copy_ref=o_ref)

    hid = jnp.dot(h_scr[...], w1_ref[...], preferred_element_type=_F32)
    hid = jnp.square(jnp.maximum(hid, 0.0)).astype(_BF16)
    part = jnp.dot(hid, w2_ref[...], preferred_element_type=_F32)
    o_ref[...] += mod_ref[0, _GATE2:_GATE2 + 1, :] * part


def _ffn(x2, mod3, g_ffn, w1, w2, *, seq, tm=512, tf=1024):
    M, D = x2.shape
    F = w1.shape[1]
    tiles_per_seq = seq // tm
    row = lambda m, f: (m, 0)
    return pl.pallas_call(
        _ffn_kernel,
        grid=(M // tm, F // tf),
        in_specs=[pl.BlockSpec((tm, D), row),
                  pl.BlockSpec((1, N_MOD, D), lambda m, f: (m // tiles_per_seq, 0, 0)),
                  pl.BlockSpec((1, D), lambda m, f: (0, 0)),
                  pl.BlockSpec((D, tf), lambda m, f: (0, f)),
                  pl.BlockSpec((tf, D), lambda m, f: (f, 0))],
        out_specs=pl.BlockSpec((tm, D), row),
        out_shape=jax.ShapeDtypeStruct((M, D), _F32),
        scratch_shapes=[pltpu.VMEM((tm, D), _BF16)],
        compiler_params=_params(("parallel", "arbitrary"), 52),
        name="ffn",
    )(x2, mod3, g_ffn, w1, w2)


def kernel(x, c, w_ada, b_ada, g_mix, w_in, b_f, g_q, g_k, g_sgu, w_s, b_s, w_out, g_ffn,
           w_ff1, w_ff2):
    B, S, D = x.shape
    depth = w_ada.shape[0]
    H = b_f.shape[1]
    G = w_s.shape[1]
    attn_w = H * HEAD_DIM
    gmlp_w = G * HEAD_DIM
    x2 = x.reshape(B * S, D)
    for l in range(depth):
        o_f = 3 * attn_w
        o_u = o_f + H
        assert attn_w == gmlp_w and w_in.shape[2] == o_u + 2 * gmlp_w
        mod, w5, w_f = _adaln(c, w_ada[l], b_ada[l], w_in[l].T, o_f=o_f, o_u=o_u)
        mod3 = mod.reshape(B, N_MOD, D)
        b_f_pad = jnp.pad(b_f[l], (0, LANES - H)).reshape(1, LANES)

        q, k, v, sgu, flog = _inproj(
            x2, mod3, g_mix[l].reshape(1, D), w5, w_f, g_q[l].reshape(1, HEAD_DIM),
            g_k[l].reshape(1, HEAD_DIM), g_sgu[l].reshape(1, gmlp_w), w_s[l], b_s[l], seq=S)

        fcol, frow = _fcumsum(flog, b_f_pad, batch=B, seq=S, n_heads=H)
        attn, w1b, w2b, wob = _attention(
            q.reshape(B, S, attn_w), k.reshape(B, S, attn_w), v.reshape(B, S, attn_w),
            fcol.reshape(B, S, LANES), frow, (w_ff1[l], w_ff2[l], w_out[l]))
        x2 = _outproj(attn.reshape(B * S, attn_w), sgu, wob, x2, mod3, seq=S)
        x2 = _ffn(x2, mod3, g_ffn[l].reshape(1, D), w1b, w2b, seq=S)
    return x2.reshape(B, S, D)
```

```python
import functools

import jax
import jax.numpy as jnp
import numpy as np
from jax import lax
from jax.experimental import pallas as pl
from jax.experimental.pallas import tpu as pltpu

_F32 = jnp.float32
_BF16 = jnp.bfloat16

HEAD_DIM = 128
CHUNK = 128
N_MOD = 6
EPS = 1e-6
LANES = 128
_LOG2E = np.float32(np.log2(np.e))

_SHIFT1, _SCALE1, _GATE1, _SHIFT2, _SCALE2, _GATE2 = range(6)

_MIB = 1024 * 1024
_VMEM_MIB = {"adaln": 40, "inproj": 56, "fcumsum": 32, "attention": 48, "mixffn": 60}


def _params(semantics, vmem_mib):
    return pltpu.CompilerParams(dimension_semantics=semantics,
                                vmem_limit_bytes=vmem_mib * _MIB)


def _adaln_kernel(c_ref, w_ref, b_ref, win_ref, o_ref, w5_ref, wf_ref, *, o_f, o_u):
    c = c_ref[...]
    c_act = (c * jax.nn.sigmoid(c)).astype(_BF16)
    w = w_ref[...].astype(_BF16)
    o_ref[...] = jnp.dot(c_act, w, preferred_element_type=_F32) + b_ref[...]

    w5_ref[0:o_f, :] = win_ref[0:o_f, :].astype(w5_ref.dtype)
    w5_ref[o_f:, :] = win_ref[o_u:, :].astype(w5_ref.dtype)
    row = lax.broadcasted_iota(jnp.int32, wf_ref.shape, 0)
    wf_ref[...] = jnp.where(row < o_u - o_f, win_ref[o_f:o_f + LANES, :], 0.0).astype(wf_ref.dtype)


def _adaln(c, w_ada, b_ada, w_in_t, *, o_f, o_u, n_steps=16):
    B, D = c.shape
    N = w_ada.shape[1]
    cols = w_in_t.shape[0]
    tn, td = N // n_steps, D // n_steps
    assert tn * n_steps == N and tn % LANES == 0 and td * n_steps == D and td % LANES == 0
    n_keep = cols - (o_u - o_f)
    dcol = lambda n: (0, n)
    return pl.pallas_call(
        functools.partial(_adaln_kernel, o_f=o_f, o_u=o_u),
        grid=(n_steps,),
        in_specs=[pl.BlockSpec((B, D), lambda n: (0, 0)),
                  pl.BlockSpec((D, tn), dcol),
                  pl.BlockSpec((1, tn), dcol),
                  pl.BlockSpec((cols, td), dcol)],
        out_specs=[pl.BlockSpec((B, tn), dcol),
                   pl.BlockSpec((n_keep, td), dcol),
                   pl.BlockSpec((LANES, td), dcol)],
        out_shape=[jax.ShapeDtypeStruct((B, N), _F32),
                   jax.ShapeDtypeStruct((n_keep, D), _BF16),
                   jax.ShapeDtypeStruct((LANES, D), _BF16)],
        compiler_params=_params(("parallel",), _VMEM_MIB["adaln"]),
        name="adaln",
    )(c, w_ada, b_ada.reshape(1, N), w_in_t)


_NORM_ROWS = 16


def _norm_modulate(x_ref, g, shift, scale, h_ref, copy_ref=None):
    one_plus_scale = 1.0 + scale
    for r in range(0, x_ref.shape[0], _NORM_ROWS):
        rows = slice(r, r + _NORM_ROWS)
        x = x_ref[rows, :]
        ms = jnp.mean(x * x, axis=-1, keepdims=True)
        y = x * lax.rsqrt(ms + EPS) * g
        h_ref[rows, :] = (y * one_plus_scale + shift).astype(h_ref.dtype)
        if copy_ref is not None:
            copy_ref[rows, :] = x


def _inproj_kernel(x_ref, mod_ref, gmix_ref, w_ref, wf_ref, gq_ref, gk_ref, gsgu_ref, ws_ref,
                   bs_ref, q_ref, k_ref, v_ref, sgu_ref, f_ref, h_scr, vn_scr, z_scr, *, q_scale):
    width = q_ref.shape[-1]
    n_heads = width // HEAD_DIM
    sec_q, sec_k, sec_v, sec_u, sec_vg = range(5)

    _norm_modulate(x_ref, gmix_ref[...], mod_ref[0, _SHIFT1:_SHIFT1 + 1, :],
                   mod_ref[0, _SCALE1:_SCALE1 + 1, :], h_scr)
    nt =(((1,), (1,)), ((), ()))
    f_ref[...] = lax.dot_general(h_scr[...], wf_ref[...], nt, preferred_element_type=_F32)

    def section(idx):
        return lax.dot_general(h_scr[...], w_ref[idx * width:(idx + 1) * width, :], nt,
                               preferred_element_type=_F32)

    def head_rms(acc, o_ref, g, scale):
        for hh in range(n_heads):
            sl = slice(hh * HEAD_DIM, (hh + 1) * HEAD_DIM)
            blk = acc[:, sl]
            ms = jnp.mean(blk * blk, axis=-1, keepdims=True)
            y = blk * lax.rsqrt(ms + EPS) * g
            if scale is not None:
                y = y * scale
            o_ref[:, sl] = y.astype(o_ref.dtype)

    acc = section(sec_vg)
    for gg in range(n_heads):
        sl = slice(gg * HEAD_DIM, (gg + 1) * HEAD_DIM)
        blk = jax.nn.gelu(acc[:, sl])
        mu = jnp.mean(blk, axis=-1, keepdims=True)
        xc = blk - mu
        var = jnp.mean(xc * xc, axis=-1, keepdims=True)
        y = xc * lax.rsqrt(var + EPS) * gsgu_ref[:, sl]
        vn_scr[:, sl] = y.astype(vn_scr.dtype)

    r = lax.broadcasted_iota(jnp.int32, (CHUNK, CHUNK), 0)
    cidx = lax.broadcasted_iota(jnp.int32, (CHUNK, CHUNK), 1)
    lower = cidx <= r
    for gg in range(n_heads):
        sl = slice(gg * HEAD_DIM, (gg + 1) * HEAD_DIM)
        w = jnp.where(lower, ws_ref[gg], 0.0).astype(_BF16)
        bcol = jnp.broadcast_to(bs_ref[gg:gg + 1, :], (CHUNK, CHUNK)).T
        for ci in range(x_ref.shape[0] // CHUNK):
            rs = slice(ci * CHUNK, (ci + 1) * CHUNK)
            z_scr[rs, sl] = jnp.dot(w, vn_scr[rs, sl], preferred_element_type=_F32) + bcol

    head_rms(section(sec_q), q_ref, gq_ref[...], q_scale)
    head_rms(section(sec_k), k_ref, gk_ref[...], None)
    sgu_ref[...] = (jax.nn.gelu(section(sec_u)) * z_scr[...]).astype(sgu_ref.dtype)
    v_ref[...] = section(sec_v).astype(v_ref.dtype)


def _inproj(x2, mod3, g_mix, w5, w_f, g_q, g_k, g_sgu, w_s, b_s, *, seq, tm=512):
    M, D = x2.shape
    n_sec = 5
    width = w5.shape[0] // n_sec
    tiles_per_seq = seq // tm
    row = lambda m: (m, 0)
    const2 = lambda m: (0, 0)
    resident = pl.Buffered(1)
    act = jax.ShapeDtypeStruct((M, width), _BF16)
    return pl.pallas_call(
        functools.partial(_inproj_kernel, q_scale=np.float32(_LOG2E / np.sqrt(HEAD_DIM))),
        grid=(M // tm,),
        in_specs=[pl.BlockSpec((tm, D), row),
                  pl.BlockSpec((1, N_MOD, D), lambda m: (m // tiles_per_seq, 0, 0)),
                  pl.BlockSpec((1, D), const2),
                  pl.BlockSpec((n_sec * width, D), const2, pipeline_mode=resident),
                  pl.BlockSpec((LANES, D), const2, pipeline_mode=resident),
                  pl.BlockSpec((1, HEAD_DIM), const2),
                  pl.BlockSpec((1, HEAD_DIM), const2),
                  pl.BlockSpec((1, width), const2),
                  pl.BlockSpec(w_s.shape, lambda m: (0, 0, 0)),
                  pl.BlockSpec(b_s.shape, const2)],
        out_specs=[pl.BlockSpec((tm, width), row)] * 4 + [pl.BlockSpec((tm, LANES), row)],
        out_shape=[act] * 4 + [jax.ShapeDtypeStruct((M, LANES), _F32)],
        scratch_shapes=[pltpu.VMEM((tm, D), _BF16), pltpu.VMEM((tm, width), _BF16),
                        pltpu.VMEM((tm, width), _F32)],
        compiler_params=_params(("parallel",), _VMEM_MIB["inproj"]),
        name="inproj",
    )(x2, mod3, g_mix, w5, w_f, g_q, g_k, g_sgu, w_s, b_s)


def _fcumsum_kernel(f_ref, bf_ref, fcol_ref, frow_ref, *, n_heads):
    S = f_ref.shape[0]
    r = lax.broadcasted_iota(jnp.int32, (CHUNK, CHUNK), 0)
    cidx = lax.broadcasted_iota(jnp.int32, (CHUNK, CHUNK), 1)
    tri = jnp.where(cidx <= r, 1.0, 0.0).astype(_BF16)
    carry = jnp.zeros((1, LANES), _F32)
    for ci in range(S // CHUNK):
        sl = slice(ci * CHUNK, (ci + 1) * CHUNK)
        lf = jax.nn.log_sigmoid(f_ref[sl, :] + bf_ref[...])
        hi = lf.astype(_BF16)
        r1 = lf - hi.astype(_F32)
        mid = r1.astype(_BF16)
        lo = (r1 - mid.astype(_F32)).astype(_BF16)
        cs = (jnp.dot(tri, hi, preferred_element_type=_F32)
              + jnp.dot(tri, mid, preferred_element_type=_F32)
              + jnp.dot(tri, lo, preferred_element_type=_F32))
        F = cs + carry
        F2 = F * _LOG2E
        fcol_ref[sl, :] = F2
        frow_ref[0, :, sl] = F2.T[0:n_heads, :]
        carry = F[CHUNK - 1:CHUNK, :]


def _fcumsum(flog, b_f_pad, *, batch, seq, n_heads):
    return pl.pallas_call(
        functools.partial(_fcumsum_kernel, n_heads=n_heads),
        grid=(batch,),
        in_specs=[pl.BlockSpec((seq, LANES), lambda b: (b, 0)),
                  pl.BlockSpec((1, LANES), lambda b: (0, 0))],
        out_specs=[pl.BlockSpec((seq, LANES), lambda b: (b, 0)),
                   pl.BlockSpec((1, n_heads, seq), lambda b: (b, 0, 0))],
        out_shape=[jax.ShapeDtypeStruct((batch * seq, LANES), _F32),
                   jax.ShapeDtypeStruct((batch, n_heads, seq), _F32)],
        compiler_params=_params(("parallel",), _VMEM_MIB["fcumsum"]),
        name="fcumsum",
    )(flog, b_f_pad)


def _attn_kernel(q_ref, k_ref, v_ref, fcol_ref, frow_ref, w1_ref, w2_ref, w3_ref,
                 o_ref, w1b_ref, w2b_ref, w3b_ref, v2_scr, s_scr, *, tq, hps):
    w1b_ref[...] = w1_ref[...].astype(w1b_ref.dtype)
    w2b_ref[...] = w2_ref[...].astype(w2b_ref.dtype)
    w3b_ref[...] = w3_ref[...].astype(w3b_ref.dtype)

    hp = pl.program_id(1)
    S = q_ref.shape[1]
    n_blk = S // tq
    hsl = [slice(hh * HEAD_DIM, (hh + 1) * HEAD_DIM) for hh in range(hps)]
    den = slice(HEAD_DIM, 2 * HEAD_DIM)
    tiles = [(i, j) for i in range(n_blk) for j in range(i + 1)]

    for hh in range(hps):
        v2_scr[hh, :, 0:HEAD_DIM] = v_ref[0, :, hsl[hh]]
        v2_scr[hh, :, den] = jnp.ones((S, HEAD_DIM), _BF16)

    lane = lax.broadcasted_iota(jnp.int32, (tq, LANES), 1)
    rows = lax.broadcasted_iota(jnp.int32, (tq, tq), 0)
    cols = lax.broadcasted_iota(jnp.int32, (tq, tq), 1)
    causal = cols <= rows

    half = tq // 2

    def parts(i, j):
        if j < i:
            return [(0, tq, 0, tq)]
        return [(0, half, 0, half), (half, tq, 0, tq)]

    def scores(t):
        i, j = tiles[t]
        for hh in range(hps):
            for r0, r1, c0, c1 in parts(i, j):
                s = lax.dot_general(q_ref[0, i * tq + r0:i * tq + r1, hsl[hh]],
                                    k_ref[0, j * tq + c0:j * tq + c1, hsl[hh]],
                                    (((1,), (1,)), ((), ())), preferred_element_type=_F32)
                s_scr[t % 2, hh, r0:r1, c0:c1] = (
                    s - frow_ref[0, 0, hh:hh + 1, j * tq + c0:j * tq + c1])

    scores(0)
    fq = m = acc = None
    for t, (i, j) in enumerate(tiles):
        if t + 1 < len(tiles):
            scores(t + 1)
        if j == 0:
            fc = fcol_ref[0, i * tq:(i + 1) * tq, :]
            fq = [jnp.sum(jnp.where(lane == hp * hps + hh, fc, 0.0), axis=-1, keepdims=True)
                  for hh in range(hps)]
            m = [None] * hps
            acc = [None] * hps
        for hh in range(hps):
            for r0, r1, c0, c1 in parts(i, j):
                a = s_scr[t % 2, hh, r0:r1, c0:c1]
                if j == i:
                    a = jnp.where(causal[r0:r1, c0:c1], a, -jnp.inf)
                fq_r = fq[hh][r0:r1, :]
                m_tile = jnp.max(a, axis=-1, keepdims=True) + fq_r
                m_new = m_tile if j == 0 else jnp.maximum(m[hh][r0:r1, :], m_tile)
                p = jnp.exp2(a - (m_new - fq_r)).astype(_BF16)
                pv = jnp.dot(p, v2_scr[hh, j * tq + c0:j * tq + c1, :],
                             preferred_element_type=_F32)
                acc_new = pv if j == 0 else (
                    jnp.exp2(m[hh][r0:r1, :] - m_new) * acc[hh][r0:r1, :] + pv)
                if j == i:
                    o_ref[0, i * tq + r0:i * tq + r1, hsl[hh]] = (
                        acc_new[:, 0:HEAD_DIM] / acc_new[:, den]).astype(o_ref.dtype)
                else:
                    m[hh], acc[hh] = m_new, acc_new


def _attention(q, k, v, fcol, frow, weights, *, tq=512, hps=2):
    B, S, W = q.shape
    H = W // HEAD_DIM
    n_hp = H // hps
    n_steps = B * n_hp
    heads = lambda b, h: (b, 0, h)
    slab = lambda b, h: (b * n_hp + h, 0)
    w_specs = []
    for w in weights:
        rows = w.shape[0] // n_steps
        assert rows * n_steps == w.shape[0]
        w_specs.append(pl.BlockSpec((rows, w.shape[1]), slab))
    return pl.pallas_call(
        functools.partial(_attn_kernel, tq=tq, hps=hps),
        grid=(B, n_hp),
        in_specs=[pl.BlockSpec((1, S, hps * HEAD_DIM), heads),
                  pl.BlockSpec((1, S, hps * HEAD_DIM), heads),
                  pl.BlockSpec((1, S, hps * HEAD_DIM), heads),
                  pl.BlockSpec((1, S, LANES), lambda b, h: (b, 0, 0)),
                  pl.BlockSpec((1, 1, hps, S), lambda b, h: (b, h, 0, 0))] + w_specs,
        out_specs=[pl.BlockSpec((1, S, hps * HEAD_DIM), heads)] + w_specs,
        out_shape=[jax.ShapeDtypeStruct((B, S, W), _BF16)]
                  + [jax.ShapeDtypeStruct(w.shape, _BF16) for w in weights],
        scratch_shapes=[pltpu.VMEM((hps, S, 2 * HEAD_DIM), _BF16),
                        pltpu.VMEM((2, hps, tq, tq), _F32)],
        compiler_params=_params(("parallel", "parallel"), _VMEM_MIB["attention"]),
        name="attention",
    )(q, k, v, fcol, frow.reshape(B, n_hp, hps, S), *weights)


def _mixffn_kernel(a_ref, s_ref, wo_ref, x_ref, mod_ref, g_ref, w1_ref, w2_ref, o_ref, h_scr):
    f = pl.program_id(1)

    @pl.when(f == 0)
    def _():
        wa = a_ref.shape[1]
        mix = (jnp.dot(a_ref[...], wo_ref[0:wa, :], preferred_element_type=_F32)
               + jnp.dot(s_ref[...], wo_ref[wa:, :], preferred_element_type=_F32))
        o_ref[...] = x_ref[...] + mod_ref[0, _GATE1:_GATE1 + 1, :] * mix
        _norm_modulate(o_ref, g_ref[...], mod_ref[0, _SHIFT2:_SHIFT2 + 1, :],
                       mod_ref[0, _SCALE2:_SCALE2 + 1, :], h_scr)

    hid = jnp.dot(h_scr[...], w1_ref[...], preferred_element_type=_F32)
    hid = jnp.square(jnp.maximum(hid, 0.0)).astype(_BF16)
    part = jnp.dot(hid, w2_ref[...], preferred_element_type=_F32)
    o_ref[...] += mod_ref[0, _GATE2:_GATE2 + 1, :] * part


def _mixffn(attn, sgu, w_out, x2, mod3, g_ffn, w1, w2, *, seq, tm=512, tf=1024):
    M, D = x2.shape
    wa, ws = attn.shape[1], sgu.shape[1]
    F = w1.shape[1]
    tiles_per_seq = seq // tm
    row = lambda m, f: (m, 0)
    const2 = lambda m, f: (0, 0)
    return pl.pallas_call(
        _mixffn_kernel,
        grid=(M // tm, F // tf),
        in_specs=[pl.BlockSpec((tm, wa), row),
                  pl.BlockSpec((tm, ws), row),
                  pl.BlockSpec((wa + ws, D), const2, pipeline_mode=pl.Buffered(1)),
                  pl.BlockSpec((tm, D), row),
                  pl.BlockSpec((1, N_MOD, D), lambda m, f: (m // tiles_per_seq, 0, 0)),
                  pl.BlockSpec((1, D), const2),
                  pl.BlockSpec((D, tf), lambda m, f: (0, f)),
                  pl.BlockSpec((tf, D), lambda m, f: (f, 0))],
        out_specs=pl.BlockSpec((tm, D), row),
        out_shape=jax.ShapeDtypeStruct((M, D), _F32),
        scratch_shapes=[pltpu.VMEM((tm, D), _BF16)],
        compiler_params=_params(("parallel", "arbitrary"), _VMEM_MIB["mixffn"]),
        name="mixffn",
    )(attn, sgu, w_out, x2, mod3, g_ffn, w1, w2)


def kernel(x, c, w_ada, b_ada, g_mix, w_in, b_f, g_q, g_k, g_sgu, w_s, b_s, w_out, g_ffn,
           w_ff1, w_ff2):
    B, S, D = x.shape
    depth = w_ada.shape[0]
    H = b_f.shape[1]
    G = w_s.shape[1]
    attn_w = H * HEAD_DIM
    gmlp_w = G * HEAD_DIM
    x2 = x.reshape(B * S, D)
    for l in range(depth):
        o_f = 3 * attn_w
        o_u = o_f + H
        assert attn_w == gmlp_w and w_in.shape[2] == o_u + 2 * gmlp_w
        mod, w5, w_f = _adaln(c, w_ada[l], b_ada[l], w_in[l].T, o_f=o_f, o_u=o_u)
        mod3 = mod.reshape(B, N_MOD, D)
        b_f_pad = jnp.pad(b_f[l], (0, LANES - H)).reshape(1, LANES)

        q, k, v, sgu, flog = _inproj(
            x2, mod3, g_mix[l].reshape(1, D), w5, w_f, g_q[l].reshape(1, HEAD_DIM),
            g_k[l].reshape(1, HEAD_DIM), g_sgu[l].reshape(1, gmlp_w), w_s[l], b_s[l], seq=S)

        fcol, frow = _fcumsum(flog, b_f_pad, batch=B, seq=S, n_heads=H)
        attn, w1b, w2b, wob = _attention(
            q.reshape(B, S, attn_w), k.reshape(B, S, attn_w), v.reshape(B, S, attn_w),
            fcol.reshape(B, S, LANES), frow, (w_ff1[l], w_ff2[l], w_out[l]))
        x2 = _mixffn(attn.reshape(B * S, attn_w), sgu, wob, x2, mod3, g_ffn[l].reshape(1, D),
                     w1b, w2b, seq=S)
    return x2.reshape(B, S, D)
```

```python
import functools

import jax
import jax.numpy as jnp
import numpy as np
from jax import lax
from jax.experimental import pallas as pl
from jax.experimental.pallas import tpu as pltpu

_F32 = jnp.float32
_BF16 = jnp.bfloat16

HEAD_DIM = 128
CHUNK = 128
N_MOD = 6
EPS = 1e-6
LANES = 128
_LOG2E = np.float32(np.log2(np.e))

_SHIFT1, _SCALE1, _GATE1, _SHIFT2, _SCALE2, _GATE2 = range(6)

_MIB = 1024 * 1024
_VMEM_MIB = {"adaln": 40, "inproj": 56, "fcumsum": 32, "attention": 48, "outproj": 48, "ffn": 52}


def _params(semantics, vmem_mib):
    return pltpu.CompilerParams(dimension_semantics=semantics,
                                vmem_limit_bytes=vmem_mib * _MIB)


def _adaln_kernel(c_ref, w_ref, b_ref, win_ref, o_ref, w5_ref, wf_ref, *, o_f, o_u):
    c = c_ref[...]
    c_act = (c * jax.nn.sigmoid(c)).astype(_BF16)
    w = w_ref[...].astype(_BF16)
    o_ref[...] = jnp.dot(c_act, w, preferred_element_type=_F32) + b_ref[...]

    w5_ref[0:o_f, :] = win_ref[0:o_f, :].astype(w5_ref.dtype)
    w5_ref[o_f:, :] = win_ref[o_u:, :].astype(w5_ref.dtype)
    row = lax.broadcasted_iota(jnp.int32, wf_ref.shape, 0)
    wf_ref[...] = jnp.where(row < o_u - o_f, win_ref[o_f:o_f + LANES, :], 0.0).astype(wf_ref.dtype)


def _adaln(c, w_ada, b_ada, w_in_t, *, o_f, o_u, n_steps=16):
    B, D = c.shape
    N = w_ada.shape[1]
    cols = w_in_t.shape[0]
    tn, td = N // n_steps, D // n_steps
    assert tn * n_steps == N and tn % LANES == 0 and td * n_steps == D and td % LANES == 0
    n_keep = cols - (o_u - o_f)
    dcol = lambda n: (0, n)
    return pl.pallas_call(
        functools.partial(_adaln_kernel, o_f=o_f, o_u=o_u),
        grid=(n_steps,),
        in_specs=[pl.BlockSpec((B, D), lambda n: (0, 0)),
                  pl.BlockSpec((D, tn), dcol),
                  pl.BlockSpec((1, tn), dcol),
                  pl.BlockSpec((cols, td), dcol)],
        out_specs=[pl.BlockSpec((B, tn), dcol),
                   pl.BlockSpec((n_keep, td), dcol),
                   pl.BlockSpec((LANES, td), dcol)],
        out_shape=[jax.ShapeDtypeStruct((B, N), _F32),
                   jax.ShapeDtypeStruct((n_keep, D), _BF16),
                   jax.ShapeDtypeStruct((LANES, D), _BF16)],
        compiler_params=_params(("parallel",), _VMEM_MIB["adaln"]),
        name="adaln",
    )(c, w_ada, b_ada.reshape(1, N), w_in_t)


_NORM_ROWS = 16


def _norm_modulate(x_ref, g, shift, scale, h_ref, copy_ref=None):
    one_plus_scale = 1.0 + scale
    for r in range(0, x_ref.shape[0], _NORM_ROWS):
        rows = slice(r, r + _NORM_ROWS)
        x = x_ref[rows, :]
        ms = jnp.mean(x * x, axis=-1, keepdims=True)
        y = x * lax.rsqrt(ms + EPS) * g
        h_ref[rows, :] = (y * one_plus_scale + shift).astype(h_ref.dtype)
        if copy_ref is not None:
            copy_ref[rows, :] = x


def _inproj_kernel(x_ref, mod_ref, gmix_ref, w_ref, wf_ref, gq_ref, gk_ref, gsgu_ref, ws_ref,
                   bs_ref, q_ref, k_ref, v_ref, sgu_ref, f_ref, h_scr, vn_scr, z_scr, *, q_scale):
    width = q_ref.shape[-1]
    n_heads = width // HEAD_DIM
    sec_q, sec_k, sec_v, sec_u, sec_vg = range(5)

    _norm_modulate(x_ref, gmix_ref[...], mod_ref[0, _SHIFT1:_SHIFT1 + 1, :],
                   mod_ref[0, _SCALE1:_SCALE1 + 1, :], h_scr)
    nt =(((1,), (1,)), ((), ()))
    f_ref[...] = lax.dot_general(h_scr[...], wf_ref[...], nt, preferred_element_type=_F32)

    def section(idx):
        return lax.dot_general(h_scr[...], w_ref[idx * width:(idx + 1) * width, :], nt,
                               preferred_element_type=_F32)

    def head_rms(acc, o_ref, g, scale):
        for hh in range(n_heads):
            sl = slice(hh * HEAD_DIM, (hh + 1) * HEAD_DIM)
            blk = acc[:, sl]
            ms = jnp.mean(blk * blk, axis=-1, keepdims=True)
            y = blk * lax.rsqrt(ms + EPS) * g
            if scale is not None:
                y = y * scale
            o_ref[:, sl] = y.astype(o_ref.dtype)

    acc = section(sec_vg)
    for gg in range(n_heads):
        sl = slice(gg * HEAD_DIM, (gg + 1) * HEAD_DIM)
        blk = jax.nn.gelu(acc[:, sl])
        mu = jnp.mean(blk, axis=-1, keepdims=True)
        xc = blk - mu
        var = jnp.mean(xc * xc, axis=-1, keepdims=True)
        y = xc * lax.rsqrt(var + EPS) * gsgu_ref[:, sl]
        vn_scr[:, sl] = y.astype(vn_scr.dtype)

    r = lax.broadcasted_iota(jnp.int32, (CHUNK, CHUNK), 0)
    cidx = lax.broadcasted_iota(jnp.int32, (CHUNK, CHUNK), 1)
    lower = cidx <= r
    for gg in range(n_heads):
        sl = slice(gg * HEAD_DIM, (gg + 1) * HEAD_DIM)
        w = jnp.where(lower, ws_ref[gg], 0.0).astype(_BF16)
        bcol = jnp.broadcast_to(bs_ref[gg:gg + 1, :], (CHUNK, CHUNK)).T
        for ci in range(x_ref.shape[0] // CHUNK):
            rs = slice(ci * CHUNK, (ci + 1) * CHUNK)
            z_scr[rs, sl] = jnp.dot(w, vn_scr[rs, sl], preferred_element_type=_F32) + bcol

    head_rms(section(sec_q), q_ref, gq_ref[...], q_scale)
    head_rms(section(sec_k), k_ref, gk_ref[...], None)
    sgu_ref[...] = (jax.nn.gelu(section(sec_u)) * z_scr[...]).astype(sgu_ref.dtype)
    v_ref[...] = section(sec_v).astype(v_ref.dtype)


def _inproj(x2, mod3, g_mix, w5, w_f, g_q, g_k, g_sgu, w_s, b_s, *, seq, tm=512):
    M, D = x2.shape
    n_sec = 5
    width = w5.shape[0] // n_sec
    tiles_per_seq = seq // tm
    row = lambda m: (m, 0)
    const2 = lambda m: (0, 0)
    resident = pl.Buffered(1)
    act = jax.ShapeDtypeStruct((M, width), _BF16)
    return pl.pallas_call(
        functools.partial(_inproj_kernel, q_scale=np.float32(_LOG2E / np.sqrt(HEAD_DIM))),
        grid=(M // tm,),
        in_specs=[pl.BlockSpec((tm, D), row),
                  pl.BlockSpec((1, N_MOD, D), lambda m: (m // tiles_per_seq, 0, 0)),
                  pl.BlockSpec((1, D), const2),
                  pl.BlockSpec((n_sec * width, D), const2, pipeline_mode=resident),
                  pl.BlockSpec((LANES, D), const2, pipeline_mode=resident),
                  pl.BlockSpec((1, HEAD_DIM), const2),
                  pl.BlockSpec((1, HEAD_DIM), const2),
                  pl.BlockSpec((1, width), const2),
                  pl.BlockSpec(w_s.shape, lambda m: (0, 0, 0)),
                  pl.BlockSpec(b_s.shape, const2)],
        out_specs=[pl.BlockSpec((tm, width), row)] * 4 + [pl.BlockSpec((tm, LANES), row)],
        out_shape=[act] * 4 + [jax.ShapeDtypeStruct((M, LANES), _F32)],
        scratch_shapes=[pltpu.VMEM((tm, D), _BF16), pltpu.VMEM((tm, width), _BF16),
                        pltpu.VMEM((tm, width), _F32)],
        compiler_params=_params(("parallel",), _VMEM_MIB["inproj"]),
        name="inproj",
    )(x2, mod3, g_mix, w5, w_f, g_q, g_k, g_sgu, w_s, b_s)


def _fcumsum_kernel(f_ref, bf_ref, fcol_ref, frow_ref, *, n_heads):
    S = f_ref.shape[0]
    r = lax.broadcasted_iota(jnp.int32, (CHUNK, CHUNK), 0)
    cidx = lax.broadcasted_iota(jnp.int32, (CHUNK, CHUNK), 1)
    tri = jnp.where(cidx <= r, 1.0, 0.0).astype(_BF16)
    carry = jnp.zeros((1, LANES), _F32)
    for ci in range(S // CHUNK):
        sl = slice(ci * CHUNK, (ci + 1) * CHUNK)
        lf = jax.nn.log_sigmoid(f_ref[sl, :] + bf_ref[...])
        hi = lf.astype(_BF16)
        r1 = lf - hi.astype(_F32)
        mid = r1.astype(_BF16)
        lo = (r1 - mid.astype(_F32)).astype(_BF16)
        cs = (jnp.dot(tri, hi, preferred_element_type=_F32)
              + jnp.dot(tri, mid, preferred_element_type=_F32)
              + jnp.dot(tri, lo, preferred_element_type=_F32))
        F = cs + carry
        F2 = F * _LOG2E
        fcol_ref[sl, :] = F2
        frow_ref[0, :, sl] = F2.T[0:n_heads, :]
        carry = F[CHUNK - 1:CHUNK, :]


def _fcumsum(flog, b_f_pad, *, batch, seq, n_heads):
    return pl.pallas_call(
        functools.partial(_fcumsum_kernel, n_heads=n_heads),
        grid=(batch,),
        in_specs=[pl.BlockSpec((seq, LANES), lambda b: (b, 0)),
                  pl.BlockSpec((1, LANES), lambda b: (0, 0))],
        out_specs=[pl.BlockSpec((seq, LANES), lambda b: (b, 0)),
                   pl.BlockSpec((1, n_heads, seq), lambda b: (b, 0, 0))],
        out_shape=[jax.ShapeDtypeStruct((batch * seq, LANES), _F32),
                   jax.ShapeDtypeStruct((batch, n_heads, seq), _F32)],
        compiler_params=_params(("parallel",), _VMEM_MIB["fcumsum"]),
        name="fcumsum",
    )(flog, b_f_pad)


def _attn_kernel(q_ref, k_ref, v_ref, fcol_ref, frow_ref, w1_ref, w2_ref, w3_ref,
                 o_ref, w1b_ref, w2b_ref, w3b_ref, v2_scr, s_scr, *, tq, hps):
    w1b_ref[...] = w1_ref[...].astype(w1b_ref.dtype)
    w2b_ref[...] = w2_ref[...].astype(w2b_ref.dtype)
    w3b_ref[...] = w3_ref[...].astype(w3b_ref.dtype)

    hp = pl.program_id(1)
    S = q_ref.shape[1]
    n_blk = S // tq
    hsl = [slice(hh * HEAD_DIM, (hh + 1) * HEAD_DIM) for hh in range(hps)]
    den = slice(HEAD_DIM, 2 * HEAD_DIM)
    tiles = [(i, j) for i in range(n_blk) for j in range(i + 1)]

    for hh in range(hps):
        v2_scr[hh, :, 0:HEAD_DIM] = v_ref[0, :, hsl[hh]]
        v2_scr[hh, :, den] = jnp.ones((S, HEAD_DIM), _BF16)

    lane = lax.broadcasted_iota(jnp.int32, (tq, LANES), 1)
    rows = lax.broadcasted_iota(jnp.int32, (tq, tq), 0)
    cols = lax.broadcasted_iota(jnp.int32, (tq, tq), 1)
    causal = cols <= rows

    half = tq // 2

    def parts(i, j):
        if j < i:
            return [(0, tq, 0, tq)]
        return [(0, half, 0, half), (half, tq, 0, tq)]

    def scores(t):
        i, j = tiles[t]
        for hh in range(hps):
            for r0, r1, c0, c1 in parts(i, j):
                s = lax.dot_general(q_ref[0, i * tq + r0:i * tq + r1, hsl[hh]],
                                    k_ref[0, j * tq + c0:j * tq + c1, hsl[hh]],
                                    (((1,), (1,)), ((), ())), preferred_element_type=_F32)
                s_scr[t % 2, hh, r0:r1, c0:c1] = (
                    s - frow_ref[0, 0, hh:hh + 1, j * tq + c0:j * tq + c1])

    scores(0)
    fq = m = acc = None
    for t, (i, j) in enumerate(tiles):
        if t + 1 < len(tiles):
            scores(t + 1)
        if j == 0:
            fc = fcol_ref[0, i * tq:(i + 1) * tq, :]
            fq = [jnp.sum(jnp.where(lane == hp * hps + hh, fc, 0.0), axis=-1, keepdims=True)
                  for hh in range(hps)]
            m = [None] * hps
            acc = [None] * hps
        for hh in range(hps):
            for r0, r1, c0, c1 in parts(i, j):
                a = s_scr[t % 2, hh, r0:r1, c0:c1]
                if j == i:
                    a = jnp.where(causal[r0:r1, c0:c1], a, -jnp.inf)
                fq_r = fq[hh][r0:r1, :]
                m_tile = jnp.max(a, axis=-1, keepdims=True) + fq_r
                m_new = m_tile if j == 0 else jnp.maximum(m[hh][r0:r1, :], m_tile)
                p = jnp.exp2(a - (m_new - fq_r)).astype(_BF16)
                pv = jnp.dot(p, v2_scr[hh, j * tq + c0:j * tq + c1, :],
                             preferred_element_type=_F32)
                acc_new = pv if j == 0 else (
                    jnp.exp2(m[hh][r0:r1, :] - m_new) * acc[hh][r0:r1, :] + pv)
                if j == i:
                    o_ref[0, i * tq + r0:i * tq + r1, hsl[hh]] = (
                        acc_new[:, 0:HEAD_DIM] / acc_new[:, den]).astype(o_ref.dtype)
                else:
                    m[hh], acc[hh] = m_new, acc_new


def _attention(q, k, v, fcol, frow, weights, *, tq=512, hps=2):
    B, S, W = q.shape
    H = W // HEAD_DIM
    n_hp = H // hps
    n_steps = B * n_hp
    heads = lambda b, h: (b, 0, h)
    slab = lambda b, h: (b * n_hp + h, 0)
    w_specs = []
    for w in weights:
        rows = w.shape[0] // n_steps
        assert rows * n_steps == w.shape[0]
        w_specs.append(pl.BlockSpec((rows, w.shape[1]), slab))
    return pl.pallas_call(
        functools.partial(_attn_kernel, tq=tq, hps=hps),
        grid=(B, n_hp),
        in_specs=[pl.BlockSpec((1, S, hps * HEAD_DIM), heads),
                  pl.BlockSpec((1, S, hps * HEAD_DIM), heads),
                  pl.BlockSpec((1, S, hps * HEAD_DIM), heads),
                  pl.BlockSpec((1, S, LANES), lambda b, h: (b, 0, 0)),
                  pl.BlockSpec((1, 1, hps, S), lambda b, h: (b, h, 0, 0))] + w_specs,
        out_specs=[pl.BlockSpec((1, S, hps * HEAD_DIM), heads)] + w_specs,
        out_shape=[jax.ShapeDtypeStruct((B, S, W), _BF16)]
                  + [jax.ShapeDtypeStruct(w.shape, _BF16) for w in weights],
        scratch_shapes=[pltpu.VMEM((hps, S, 2 * HEAD_DIM), _BF16),
                        pltpu.VMEM((2, hps, tq, tq), _F32)],
        compiler_params=_params(("parallel", "parallel"), _VMEM_MIB["attention"]),
        name="attention",
    )(q, k, v, fcol, frow.reshape(B, n_hp, hps, S), *weights)


def _outproj_kernel(a_ref, s_ref, w_ref, x_ref, mod_ref, o_ref):
    wa = a_ref.shape[1]
    mix = (jnp.dot(a_ref[...], w_ref[0:wa, :], preferred_element_type=_F32)
           + jnp.dot(s_ref[...], w_ref[wa:, :], preferred_element_type=_F32))
    o_ref[...] = x_ref[...] + mod_ref[0, _GATE1:_GATE1 + 1, :] * mix


def _outproj(attn, sgu, w_out, x2, mod3, *, seq, tm=512):
    M, D = x2.shape
    wa, ws = attn.shape[1], sgu.shape[1]
    tiles_per_seq = seq // tm
    row = lambda m: (m, 0)
    return pl.pallas_call(
        _outproj_kernel,
        grid=(M // tm,),
        in_specs=[pl.BlockSpec((tm, wa), row),
                  pl.BlockSpec((tm, ws), row),
                  pl.BlockSpec((wa + ws, D), lambda m: (0, 0)),
                  pl.BlockSpec((tm, D), row),
                  pl.BlockSpec((1, N_MOD, D), lambda m: (m // tiles_per_seq, 0, 0))],
        out_specs=pl.BlockSpec((tm, D), row),
        out_shape=jax.ShapeDtypeStruct((M, D), _F32),
        compiler_params=_params(("parallel",), _VMEM_MIB["outproj"]),
        name="outproj",
    )(attn, sgu, w_out, x2, mod3)


def _ffn_kernel(x_ref, mod_ref, g_ref, w1_ref, w2_ref, o_ref, h_scr):
    f = pl.program_id(1)

    @pl.when(f == 0)
    def _():
        _norm_modulate(x_ref, g_ref[...], mod_ref[0, _SHIFT2:_SHIFT2 + 1, :],
                       mod_ref[0, _SCALE2:_SCALE2 + 1, :], h_scr, copy_ref=o_ref)

    hid = jnp.dot(h_scr[...], w1_ref[...], preferred_element_type=_F32)
    hid = jnp.square(jnp.maximum(hid, 0.0)).astype(_BF16)
    part = jnp.dot(hid, w2_ref[...], preferred_element_type=_F32)
    o_ref[...] += mod_ref[0, _GATE2:_GATE2 + 1, :] * part


def _ffn(x2, mod3, g_ffn, w1, w2, *, seq, tm=512, tf=1024):
    M, D = x2.shape
    F = w1.shape[1]
    tiles_per_seq = seq // tm
    row = lambda m, f: (m, 0)
    return pl.pallas_call(
        _ffn_kernel,
        grid=(M // tm, F // tf),
        in_specs=[pl.BlockSpec((tm, D), row),
                  pl.BlockSpec((1, N_MOD, D), lambda m, f: (m // tiles_per_seq, 0, 0)),
                  pl.BlockSpec((1, D), lambda m, f: (0, 0)),
                  pl.BlockSpec((D, tf), lambda m, f: (0, f)),
                  pl.BlockSpec((tf, D), lambda m, f: (f, 0))],
        out_specs=pl.BlockSpec((tm, D), row),
        out_shape=jax.ShapeDtypeStruct((M, D), _F32),
        scratch_shapes=[pltpu.VMEM((tm, D), _BF16)],
        compiler_params=_params(("parallel", "arbitrary"), _VMEM_MIB["ffn"]),
        name="ffn",
    )(x2, mod3, g_ffn, w1, w2)


def kernel(x, c, w_ada, b_ada, g_mix, w_in, b_f, g_q, g_k, g_sgu, w_s, b_s, w_out, g_ffn,
           w_ff1, w_ff2):
    B, S, D = x.shape
    depth = w_ada.shape[0]
    H = b_f.shape[1]
    G = w_s.shape[1]
    attn_w = H * HEAD_DIM
    gmlp_w = G * HEAD_DIM
    x2 = x.reshape(B * S, D)
    for l in range(depth):
        o_f = 3 * attn_w
        o_u = o_f + H
        assert attn_w == gmlp_w and w_in.shape[2] == o_u + 2 * gmlp_w
        mod, w5, w_f = _adaln(c, w_ada[l], b_ada[l], w_in[l].T, o_f=o_f, o_u=o_u)
        mod3 = mod.reshape(B, N_MOD, D)
        b_f_pad = jnp.pad(b_f[l], (0, LANES - H)).reshape(1, LANES)

        q, k, v, sgu, flog = _inproj(
            x2, mod3, g_mix[l].reshape(1, D), w5, w_f, g_q[l].reshape(1, HEAD_DIM),
            g_k[l].reshape(1, HEAD_DIM), g_sgu[l].reshape(1, gmlp_w), w_s[l], b_s[l], seq=S)

        fcol, frow = _fcumsum(flog, b_f_pad, batch=B, seq=S, n_heads=H)
        attn, w1b, w2b, wob = _attention(
            q.reshape(B, S, attn_w), k.reshape(B, S, attn_w), v.reshape(B, S, attn_w),
            fcol.reshape(B, S, LANES), frow, (w_ff1[l], w_ff2[l], w_out[l]))
        x2 = _outproj(attn.reshape(B * S, attn_w), sgu, wob, x2, mod3, seq=S)
        x2 = _ffn(x2, mod3, g_ffn[l].reshape(1, D), w1b, w2b, seq=S)
    return x2.reshape(B, S, D)
```

```python
import functools

import jax
import jax.numpy as jnp
import numpy as np
from jax import lax
from jax.experimental import pallas as pl
from jax.experimental.pallas import tpu as pltpu

_F32 = jnp.float32
_BF16 = jnp.bfloat16

HEAD_DIM = 128
CHUNK = 128
N_MOD = 6
EPS = 1e-6
LANES = 128
_LOG2E = np.float32(np.log2(np.e))

_SHIFT1, _SCALE1, _GATE1, _SHIFT2, _SCALE2, _GATE2 = range(6)

_MIB = 1024 * 1024
_VMEM_MIB = {"adaln": 40, "inproj": 56, "fcumsum": 32, "attention": 48, "outproj": 48, "ffn": 52}


def _params(semantics, vmem_mib):
    return pltpu.CompilerParams(dimension_semantics=semantics,
                                vmem_limit_bytes=vmem_mib * _MIB)


def _adaln_kernel(c_ref, w_ref, b_ref, win_ref, o_ref, w5_ref, wf_ref, *, o_f, o_u):
    c = c_ref[...]
    c_act = (c * jax.nn.sigmoid(c)).astype(_BF16)
    w = w_ref[...].astype(_BF16)
    o_ref[...] = jnp.dot(c_act, w, preferred_element_type=_F32) + b_ref[...]

    width = (win_ref.shape[0] - o_u) // 2
    w5_ref[0:width, :] = win_ref[o_u + width:, :].astype(w5_ref.dtype)
    w5_ref[width:2 * width, :] = win_ref[o_u:o_u + width, :].astype(w5_ref.dtype)
    w5_ref[2 * width:, :] = win_ref[0:o_f, :].astype(w5_ref.dtype)
    row = lax.broadcasted_iota(jnp.int32, wf_ref.shape, 0)
    wf_ref[...] = jnp.where(row < o_u - o_f, win_ref[o_f:o_f + LANES, :], 0.0).astype(wf_ref.dtype)


def _adaln(c, w_ada, b_ada, w_in_t, *, o_f, o_u, n_steps=16):
    B, D = c.shape
    N = w_ada.shape[1]
    cols = w_in_t.shape[0]
    tn, td = N // n_steps, D // n_steps
    assert tn * n_steps == N and tn % LANES == 0 and td * n_steps == D and td % LANES == 0
    n_keep = cols - (o_u - o_f)
    dcol = lambda n: (0, n)
    return pl.pallas_call(
        functools.partial(_adaln_kernel, o_f=o_f, o_u=o_u),
        grid=(n_steps,),
        in_specs=[pl.BlockSpec((B, D), lambda n: (0, 0)),
                  pl.BlockSpec((D, tn), dcol),
                  pl.BlockSpec((1, tn), dcol),
                  pl.BlockSpec((cols, td), dcol)],
        out_specs=[pl.BlockSpec((B, tn), dcol),
                   pl.BlockSpec((n_keep, td), dcol),
                   pl.BlockSpec((LANES, td), dcol)],
        out_shape=[jax.ShapeDtypeStruct((B, N), _F32),
                   jax.ShapeDtypeStruct((n_keep, D), _BF16),
                   jax.ShapeDtypeStruct((LANES, D), _BF16)],
        compiler_params=_params(("parallel",), _VMEM_MIB["adaln"]),
        name="adaln",
    )(c, w_ada, b_ada.reshape(1, N), w_in_t)


_NORM_ROWS = 16


def _norm_modulate(x_ref, g, shift, scale, h_ref, copy_ref=None):
    one_plus_scale = 1.0 + scale
    for r in range(0, x_ref.shape[0], _NORM_ROWS):
        rows = slice(r, r + _NORM_ROWS)
        x = x_ref[rows, :]
        ms = jnp.mean(x * x, axis=-1, keepdims=True)
        y = x * lax.rsqrt(ms + EPS) * g
        h_ref[rows, :] = (y * one_plus_scale + shift).astype(h_ref.dtype)
        if copy_ref is not None:
            copy_ref[rows, :] = x


def _inproj_kernel(x_ref, mod_ref, gmix_ref, w_ref, wf_ref, gq_ref, gk_ref, gsgu_ref, ws_ref,
                   bs_ref, q_ref, k_ref, v_ref, sgu_ref, f_ref, h_scr, vn_scr, z_scr, *, q_scale):
    width = q_ref.shape[-1]
    n_heads = width // HEAD_DIM
    sec_vg, sec_u, sec_q, sec_k, sec_v = range(5)

    _norm_modulate(x_ref, gmix_ref[...], mod_ref[0, _SHIFT1:_SHIFT1 + 1, :],
                   mod_ref[0, _SCALE1:_SCALE1 + 1, :], h_scr)
    nt =(((1,), (1,)), ((), ()))
    f_ref[...] = lax.dot_general(h_scr[...], wf_ref[...], nt, preferred_element_type=_F32)

    proj = lax.dot_general(h_scr[...], w_ref[...], nt, preferred_element_type=_F32)

    def section(idx):
        return proj[:, idx * width:(idx + 1) * width]

    def head_rms(acc, o_ref, g, scale):
        for hh in range(n_heads):
            sl = slice(hh * HEAD_DIM, (hh + 1) * HEAD_DIM)
            blk = acc[:, sl]
            ms = jnp.mean(blk * blk, axis=-1, keepdims=True)
            y = blk * lax.rsqrt(ms + EPS) * g
            if scale is not None:
                y = y * scale
            o_ref[:, sl] = y.astype(o_ref.dtype)

    acc = section(sec_vg)
    for gg in range(n_heads):
        sl = slice(gg * HEAD_DIM, (gg + 1) * HEAD_DIM)
        blk = jax.nn.gelu(acc[:, sl])
        mu = jnp.mean(blk, axis=-1, keepdims=True)
        xc = blk - mu
        var = jnp.mean(xc * xc, axis=-1, keepdims=True)
        y = xc * lax.rsqrt(var + EPS) * gsgu_ref[:, sl]
        vn_scr[:, sl] = y.astype(vn_scr.dtype)

    r = lax.broadcasted_iota(jnp.int32, (CHUNK, CHUNK), 0)
    cidx = lax.broadcasted_iota(jnp.int32, (CHUNK, CHUNK), 1)
    lower = cidx <= r
    for gg in range(n_heads):
        sl = slice(gg * HEAD_DIM, (gg + 1) * HEAD_DIM)
        w = jnp.where(lower, ws_ref[gg], 0.0).astype(_BF16)
        bcol = jnp.broadcast_to(bs_ref[gg:gg + 1, :], (CHUNK, CHUNK)).T
        for ci in range(x_ref.shape[0] // CHUNK):
            rs = slice(ci * CHUNK, (ci + 1) * CHUNK)
            z_scr[rs, sl] = jnp.dot(w, vn_scr[rs, sl], preferred_element_type=_F32) + bcol

    head_rms(section(sec_q), q_ref, gq_ref[...], q_scale)
    head_rms(section(sec_k), k_ref, gk_ref[...], None)
    sgu_ref[...] = (jax.nn.gelu(section(sec_u)) * z_scr[...]).astype(sgu_ref.dtype)
    v_ref[...] = section(sec_v).astype(v_ref.dtype)


def _inproj(x2, mod3, g_mix, w5, w_f, g_q, g_k, g_sgu, w_s, b_s, *, seq, tm=512):
    M, D = x2.shape
    n_sec = 5
    width = w5.shape[0] // n_sec
    tiles_per_seq = seq // tm
    row = lambda m: (m, 0)
    const2 = lambda m: (0, 0)
    resident = pl.Buffered(1)
    act = jax.ShapeDtypeStruct((M, width), _BF16)
    return pl.pallas_call(
        functools.partial(_inproj_kernel, q_scale=np.float32(_LOG2E / np.sqrt(HEAD_DIM))),
        grid=(M // tm,),
        in_specs=[pl.BlockSpec((tm, D), row),
                  pl.BlockSpec((1, N_MOD, D), lambda m: (m // tiles_per_seq, 0, 0)),
                  pl.BlockSpec((1, D), const2),
                  pl.BlockSpec((n_sec * width, D), const2, pipeline_mode=resident),
                  pl.BlockSpec((LANES, D), const2, pipeline_mode=resident),
                  pl.BlockSpec((1, HEAD_DIM), const2),
                  pl.BlockSpec((1, HEAD_DIM), const2),
                  pl.BlockSpec((1, width), const2),
                  pl.BlockSpec(w_s.shape, lambda m: (0, 0, 0)),
                  pl.BlockSpec(b_s.shape, const2)],
        out_specs=[pl.BlockSpec((tm, width), row)] * 4 + [pl.BlockSpec((tm, LANES), row)],
        out_shape=[act] * 4 + [jax.ShapeDtypeStruct((M, LANES), _F32)],
        scratch_shapes=[pltpu.VMEM((tm, D), _BF16), pltpu.VMEM((tm, width), _BF16),
                        pltpu.VMEM((tm, width), _F32)],
        compiler_params=_params(("parallel",), _VMEM_MIB["inproj"]),
        name="inproj",
    )(x2, mod3, g_mix, w5, w_f, g_q, g_k, g_sgu, w_s, b_s)


def _fcumsum_kernel(f_ref, bf_ref, fcol_ref, frow_ref, *, n_heads):
    S = f_ref.shape[0]
    r = lax.broadcasted_iota(jnp.int32, (CHUNK, CHUNK), 0)
    cidx = lax.broadcasted_iota(jnp.int32, (CHUNK, CHUNK), 1)
    tri = jnp.where(cidx <= r, 1.0, 0.0).astype(_BF16)
    carry = jnp.zeros((1, LANES), _F32)
    for ci in range(S // CHUNK):
        sl = slice(ci * CHUNK, (ci + 1) * CHUNK)
        lf = jax.nn.log_sigmoid(f_ref[sl, :] + bf_ref[...])
        hi = lf.astype(_BF16)
        r1 = lf - hi.astype(_F32)
        mid = r1.astype(_BF16)
        lo = (r1 - mid.astype(_F32)).astype(_BF16)
        cs = (jnp.dot(tri, hi, preferred_element_type=_F32)
              + jnp.dot(tri, mid, preferred_element_type=_F32)
              + jnp.dot(tri, lo, preferred_element_type=_F32))
        F = cs + carry
        F2 = F * _LOG2E
        fcol_ref[sl, :] = F2
        frow_ref[0, :, sl] = F2.T[0:n_heads, :]
        carry = F[CHUNK - 1:CHUNK, :]


def _fcumsum(flog, b_f_pad, *, batch, seq, n_heads):
    return pl.pallas_call(
        functools.partial(_fcumsum_kernel, n_heads=n_heads),
        grid=(batch,),
        in_specs=[pl.BlockSpec((seq, LANES), lambda b: (b, 0)),
                  pl.BlockSpec((1, LANES), lambda b: (0, 0))],
        out_specs=[pl.BlockSpec((seq, LANES), lambda b: (b, 0)),
                   pl.BlockSpec((1, n_heads, seq), lambda b: (b, 0, 0))],
        out_shape=[jax.ShapeDtypeStruct((batch * seq, LANES), _F32),
                   jax.ShapeDtypeStruct((batch, n_heads, seq), _F32)],
        compiler_params=_params(("parallel",), _VMEM_MIB["fcumsum"]),
        name="fcumsum",
    )(flog, b_f_pad)


def _attn_kernel(q_ref, k_ref, v_ref, fcol_ref, frow_ref, w1_ref, w2_ref, w3_ref,
                 o_ref, w1b_ref, w2b_ref, w3b_ref, v2_scr, s_scr, *, tq, hps):
    w1b_ref[...] = w1_ref[...].astype(w1b_ref.dtype)
    w2b_ref[...] = w2_ref[...].astype(w2b_ref.dtype)
    w3b_ref[...] = w3_ref[...].astype(w3b_ref.dtype)

    hp = pl.program_id(1)
    S = q_ref.shape[1]
    n_blk = S // tq
    hsl = [slice(hh * HEAD_DIM, (hh + 1) * HEAD_DIM) for hh in range(hps)]
    den = slice(HEAD_DIM, 2 * HEAD_DIM)
    tiles = [(i, j) for i in range(n_blk) for j in range(i + 1)]

    for hh in range(hps):
        v2_scr[hh, :, 0:HEAD_DIM] = v_ref[0, :, hsl[hh]]
        v2_scr[hh, :, den] = jnp.ones((S, HEAD_DIM), _BF16)

    lane = lax.broadcasted_iota(jnp.int32, (tq, LANES), 1)
    rows = lax.broadcasted_iota(jnp.int32, (tq, tq), 0)
    cols = lax.broadcasted_iota(jnp.int32, (tq, tq), 1)
    causal = cols <= rows

    half = tq // 2

    def parts(i, j):
        if j < i:
            return [(0, tq, 0, tq)]
        return [(0, half, 0, half), (half, tq, 0, tq)]

    def scores(t):
        i, j = tiles[t]
        for hh in range(hps):
            for r0, r1, c0, c1 in parts(i, j):
                s = lax.dot_general(q_ref[0, i * tq + r0:i * tq + r1, hsl[hh]],
                                    k_ref[0, j * tq + c0:j * tq + c1, hsl[hh]],
                                    (((1,), (1,)), ((), ())), preferred_element_type=_F32)
                s_scr[t % 2, hh, r0:r1, c0:c1] = (
                    s - frow_ref[0, 0, hh:hh + 1, j * tq + c0:j * tq + c1])

    scores(0)
    fq = m = acc = None
    for t, (i, j) in enumerate(tiles):
        if t + 1 < len(tiles):
            scores(t + 1)
        if j == 0:
            fc = fcol_ref[0, i * tq:(i + 1) * tq, :]
            fq = [jnp.sum(jnp.where(lane == hp * hps + hh, fc, 0.0), axis=-1, keepdims=True)
                  for hh in range(hps)]
            m = [None] * hps
            acc = [None] * hps
        for hh in range(hps):
            for r0, r1, c0, c1 in parts(i, j):
                a = s_scr[t % 2, hh, r0:r1, c0:c1]
                if j == i:
                    a = jnp.where(causal[r0:r1, c0:c1], a, -jnp.inf)
                fq_r = fq[hh][r0:r1, :]
                m_tile = jnp.max(a, axis=-1, keepdims=True) + fq_r
                m_new = m_tile if j == 0 else jnp.maximum(m[hh][r0:r1, :], m_tile)
                p = jnp.exp2(a - (m_new - fq_r)).astype(_BF16)
                pv = jnp.dot(p, v2_scr[hh, j * tq + c0:j * tq + c1, :],
                             preferred_element_type=_F32)
                acc_new = pv if j == 0 else (
                    jnp.exp2(m[hh][r0:r1, :] - m_new) * acc[hh][r0:r1, :] + pv)
                if j == i:
                    o_ref[0, i * tq + r0:i * tq + r1, hsl[hh]] = (
                        acc_new[:, 0:HEAD_DIM] / acc_new[:, den]).astype(o_ref.dtype)
                else:
                    m[hh], acc[hh] = m_new, acc_new


def _attention(q, k, v, fcol, frow, weights, *, tq=512, hps=2):
    B, S, W = q.shape
    H = W // HEAD_DIM
    n_hp = H // hps
    n_steps = B * n_hp
    heads = lambda b, h: (b, 0, h)
    slab = lambda b, h: (b * n_hp + h, 0)
    w_specs = []
    for w in weights:
        rows = w.shape[0] // n_steps
        assert rows * n_steps == w.shape[0]
        w_specs.append(pl.BlockSpec((rows, w.shape[1]), slab))
    return pl.pallas_call(
        functools.partial(_attn_kernel, tq=tq, hps=hps),
        grid=(B, n_hp),
        in_specs=[pl.BlockSpec((1, S, hps * HEAD_DIM), heads),
                  pl.BlockSpec((1, S, hps * HEAD_DIM), heads),
                  pl.BlockSpec((1, S, hps * HEAD_DIM), heads),
                  pl.BlockSpec((1, S, LANES), lambda b, h: (b, 0, 0)),
                  pl.BlockSpec((1, 1, hps, S), lambda b, h: (b, h, 0, 0))] + w_specs,
        out_specs=[pl.BlockSpec((1, S, hps * HEAD_DIM), heads)] + w_specs,
        out_shape=[jax.ShapeDtypeStruct((B, S, W), _BF16)]
                  + [jax.ShapeDtypeStruct(w.shape, _BF16) for w in weights],
        scratch_shapes=[pltpu.VMEM((hps, S, 2 * HEAD_DIM), _BF16),
                        pltpu.VMEM((2, hps, tq, tq), _F32)],
        compiler_params=_params(("parallel", "parallel"), _VMEM_MIB["attention"]),
        name="attention",
    )(q, k, v, fcol, frow.reshape(B, n_hp, hps, S), *weights)


def _outproj_kernel(a_ref, s_ref, w_ref, x_ref, mod_ref, o_ref):
    wa = a_ref.shape[1]
    mix = (jnp.dot(a_ref[...], w_ref[0:wa, :], preferred_element_type=_F32)
           + jnp.dot(s_ref[...], w_ref[wa:, :], preferred_element_type=_F32))
    o_ref[...] = x_ref[...] + mod_ref[0, _GATE1:_GATE1 + 1, :] * mix


def _outproj(attn, sgu, w_out, x2, mod3, *, seq, tm=512):
    M, D = x2.shape
    wa, ws = attn.shape[1], sgu.shape[1]
    tiles_per_seq = seq // tm
    row = lambda m: (m, 0)
    return pl.pallas_call(
        _outproj_kernel,
        grid=(M // tm,),
        in_specs=[pl.BlockSpec((tm, wa), row),
                  pl.BlockSpec((tm, ws), row),
                  pl.BlockSpec((wa + ws, D), lambda m: (0, 0)),
                  pl.BlockSpec((tm, D), row),
                  pl.BlockSpec((1, N_MOD, D), lambda m: (m // tiles_per_seq, 0, 0))],
        out_specs=pl.BlockSpec((tm, D), row),
        out_shape=jax.ShapeDtypeStruct((M, D), _F32),
        compiler_params=_params(("parallel",), _VMEM_MIB["outproj"]),
        name="outproj",
    )(attn, sgu, w_out, x2, mod3)


def _ffn_kernel(x_ref, mod_ref, g_ref, w1_ref, w2_ref, o_ref, h_scr):
    f = pl.program_id(1)

    @pl.when(f == 0)
    def _():
        _norm_modulate(x_ref, g_ref[...], mod_ref[0, _SHIFT2:_SHIFT2 + 1, :],
                       mod_ref[0, _SCALE2:_SCALE2 + 1, :], h_scr, copy_ref=o_ref)

    hid = jnp.dot(h_scr[...], w1_ref[...], preferred_element_type=_F32)
    hid = jnp.square(jnp.maximum(hid, 0.0)).astype(_BF16)
    part = jnp.dot(hid, w2_ref[...], preferred_element_type=_F32)
    o_ref[...] += mod_ref[0, _GATE2:_GATE2 + 1, :] * part


def _ffn(x2, mod3, g_ffn, w1, w2, *, seq, tm=512, tf=1024):
    M, D = x2.shape
    F = w1.shape[1]
    tiles_per_seq = seq // tm
    row = lambda m, f: (m, 0)
    return pl.pallas_call(
        _ffn_kernel,
        grid=(M // tm, F // tf),
        in_specs=[pl.BlockSpec((tm, D), row),
                  pl.BlockSpec((1, N_MOD, D), lambda m, f: (m // tiles_per_seq, 0, 0)),
                  pl.BlockSpec((1, D), lambda m, f: (0, 0)),
                  pl.BlockSpec((D, tf), lambda m, f: (0, f)),
                  pl.BlockSpec((tf, D), lambda m, f: (f, 0))],
        out_specs=pl.BlockSpec((tm, D), row),
        out_shape=jax.ShapeDtypeStruct((M, D), _F32),
        scratch_shapes=[pltpu.VMEM((tm, D), _BF16)],
        compiler_params=_params(("parallel", "arbitrary"), _VMEM_MIB["ffn"]),
        name="ffn",
    )(x2, mod3, g_ffn, w1, w2)


def kernel(x, c, w_ada, b_ada, g_mix, w_in, b_f, g_q, g_k, g_sgu, w_s, b_s, w_out, g_ffn,
           w_ff1, w_ff2):
    B, S, D = x.shape
    depth = w_ada.shape[0]
    H = b_f.shape[1]
    G = w_s.shape[1]
    attn_w = H * HEAD_DIM
    gmlp_w = G * HEAD_DIM
    x2 = x.reshape(B * S, D)
    for l in range(depth):
        o_f = 3 * attn_w
        o_u = o_f + H
        assert attn_w == gmlp_w and w_in.shape[2] == o_u + 2 * gmlp_w
        mod, w5, w_f = _adaln(c, w_ada[l], b_ada[l], w_in[l].T, o_f=o_f, o_u=o_u)
        mod3 = mod.reshape(B, N_MOD, D)
        b_f_pad = jnp.pad(b_f[l], (0, LANES - H)).reshape(1, LANES)

        q, k, v, sgu, flog = _inproj(
            x2, mod3, g_mix[l].reshape(1, D), w5, w_f, g_q[l].reshape(1, HEAD_DIM),
            g_k[l].reshape(1, HEAD_DIM), g_sgu[l].reshape(1, gmlp_w), w_s[l], b_s[l], seq=S)

        fcol, frow = _fcumsum(flog, b_f_pad, batch=B, seq=S, n_heads=H)
        attn, w1b, w2b, wob = _attention(
            q.reshape(B, S, attn_w), k.reshape(B, S, attn_w), v.reshape(B, S, attn_w),
            fcol.reshape(B, S, LANES), frow, (w_ff1[l], w_ff2[l], w_out[l]))
        x2 = _outproj(attn.reshape(B * S, attn_w), sgu, wob, x2, mod3, seq=S)
        x2 = _ffn(x2, mod3, g_ffn[l].reshape(1, D), w1b, w2b, seq=S)
    return x2.reshape(B, S, D)
```

```python
import functools

import jax
import jax.numpy as jnp
import numpy as np
from jax import lax
from jax.experimental import pallas as pl
from jax.experimental.pallas import tpu as pltpu

_F32 = jnp.float32
_BF16 = jnp.bfloat16

HEAD_DIM = 128
CHUNK = 128
N_MOD = 6
EPS = 1e-6
LANES = 128
_LOG2E = np.float32(np.log2(np.e))

_SHIFT1, _SCALE1, _GATE1, _SHIFT2, _SCALE2, _GATE2 = range(6)

_MIB = 1024 * 1024
_VMEM_MIB = {"adaln": 40, "inproj": 56, "fcumsum": 32, "attention": 48, "outproj": 48, "ffn": 52}


def _params(semantics, vmem_mib):
    return pltpu.CompilerParams(dimension_semantics=semantics,
                                vmem_limit_bytes=vmem_mib * _MIB)


def _adaln_kernel(c_ref, w_ref, b_ref, win_ref, o_ref, w6_ref, *, o_f, o_u):
    c = c_ref[...]
    c_act = (c * jax.nn.sigmoid(c)).astype(_BF16)
    w = w_ref[...].astype(_BF16)
    o_ref[...] = jnp.dot(c_act, w, preferred_element_type=_F32) + b_ref[...]

    width = (win_ref.shape[0] - o_u) // 2
    w6_ref[0:width, :] = win_ref[o_u + width:, :].astype(w6_ref.dtype)
    w6_ref[width:2 * width, :] = win_ref[o_u:o_u + width, :].astype(w6_ref.dtype)
    w6_ref[2 * width:5 * width, :] = win_ref[0:o_f, :].astype(w6_ref.dtype)
    row = lax.broadcasted_iota(jnp.int32, (LANES, w6_ref.shape[1]), 0)
    w6_ref[5 * width:, :] = jnp.where(row < o_u - o_f, win_ref[o_f:o_f + LANES, :],
                                      0.0).astype(w6_ref.dtype)


def _adaln(c, w_ada, b_ada, w_in_t, *, o_f, o_u, n_steps=16):
    B, D = c.shape
    N = w_ada.shape[1]
    cols = w_in_t.shape[0]
    tn, td = N // n_steps, D // n_steps
    assert tn * n_steps == N and tn % LANES == 0 and td * n_steps == D and td % LANES == 0
    n_rows = cols - (o_u - o_f) + LANES
    dcol = lambda n: (0, n)
    return pl.pallas_call(
        functools.partial(_adaln_kernel, o_f=o_f, o_u=o_u),
        grid=(n_steps,),
        in_specs=[pl.BlockSpec((B, D), lambda n: (0, 0)),
                  pl.BlockSpec((D, tn), dcol),
                  pl.BlockSpec((1, tn), dcol),
                  pl.BlockSpec((cols, td), dcol)],
        out_specs=[pl.BlockSpec((B, tn), dcol),
                   pl.BlockSpec((n_rows, td), dcol)],
        out_shape=[jax.ShapeDtypeStruct((B, N), _F32),
                   jax.ShapeDtypeStruct((n_rows, D), _BF16)],
        compiler_params=_params(("parallel",), _VMEM_MIB["adaln"]),
        name="adaln",
    )(c, w_ada, b_ada.reshape(1, N), w_in_t)


_NORM_ROWS = 16


def _norm_modulate(x_ref, g, shift, scale, h_ref, copy_ref=None):
    one_plus_scale = 1.0 + scale
    for r in range(0, x_ref.shape[0], _NORM_ROWS):
        rows = slice(r, r + _NORM_ROWS)
        x = x_ref[rows, :]
        ms = jnp.mean(x * x, axis=-1, keepdims=True)
        y = x * lax.rsqrt(ms + EPS) * g
        h_ref[rows, :] = (y * one_plus_scale + shift).astype(h_ref.dtype)
        if copy_ref is not None:
            copy_ref[rows, :] = x


def _inproj_kernel(x_ref, mod_ref, gmix_ref, w_ref, gq_ref, gk_ref, gsgu_ref, ws_ref,
                   bs_ref, q_ref, k_ref, v_ref, sgu_ref, f_ref, h_scr, vn_scr, z_scr, *, q_scale):
    width = q_ref.shape[-1]
    n_heads = width // HEAD_DIM
    sec_vg, sec_u, sec_q, sec_k, sec_v = range(5)

    _norm_modulate(x_ref, gmix_ref[...], mod_ref[0, _SHIFT1:_SHIFT1 + 1, :],
                   mod_ref[0, _SCALE1:_SCALE1 + 1, :], h_scr)
    proj = lax.dot_general(h_scr[...], w_ref[...], (((1,), (1,)), ((), ())),
                           preferred_element_type=_F32)
    f_ref[...] = proj[:, 5 * width:]

    def section(idx):
        return proj[:, idx * width:(idx + 1) * width]

    def head_rms(acc, o_ref, g, scale):
        for hh in range(n_heads):
            sl = slice(hh * HEAD_DIM, (hh + 1) * HEAD_DIM)
            blk = acc[:, sl]
            ms = jnp.mean(blk * blk, axis=-1, keepdims=True)
            y = blk * lax.rsqrt(ms + EPS) * g
            if scale is not None:
                y = y * scale
            o_ref[:, sl] = y.astype(o_ref.dtype)

    acc = section(sec_vg)
    for gg in range(n_heads):
        sl = slice(gg * HEAD_DIM, (gg + 1) * HEAD_DIM)
        blk = jax.nn.gelu(acc[:, sl])
        mu = jnp.mean(blk, axis=-1, keepdims=True)
        xc = blk - mu
        var = jnp.mean(xc * xc, axis=-1, keepdims=True)
        y = xc * lax.rsqrt(var + EPS) * gsgu_ref[:, sl]
        vn_scr[:, sl] = y.astype(vn_scr.dtype)

    r = lax.broadcasted_iota(jnp.int32, (CHUNK, CHUNK), 0)
    cidx = lax.broadcasted_iota(jnp.int32, (CHUNK, CHUNK), 1)
    lower = cidx <= r
    for gg in range(n_heads):
        sl = slice(gg * HEAD_DIM, (gg + 1) * HEAD_DIM)
        w = jnp.where(lower, ws_ref[gg], 0.0).astype(_BF16)
        bcol = jnp.broadcast_to(bs_ref[gg:gg + 1, :], (CHUNK, CHUNK)).T
        for ci in range(x_ref.shape[0] // CHUNK):
            rs = slice(ci * CHUNK, (ci + 1) * CHUNK)
            z_scr[rs, sl] = jnp.dot(w, vn_scr[rs, sl], preferred_element_type=_F32) + bcol

    head_rms(section(sec_q), q_ref, gq_ref[...], q_scale)
    head_rms(section(sec_k), k_ref, gk_ref[...], None)
    sgu_ref[...] = (jax.nn.gelu(section(sec_u)) * z_scr[...]).astype(sgu_ref.dtype)
    v_ref[...] = section(sec_v).astype(v_ref.dtype)


def _inproj(x2, mod3, g_mix, w6, g_q, g_k, g_sgu, w_s, b_s, *, seq, tm=512):
    M, D = x2.shape
    n_sec = 5
    width = (w6.shape[0] - LANES) // n_sec
    tiles_per_seq = seq // tm
    row = lambda m: (m, 0)
    const2 = lambda m: (0, 0)
    resident = pl.Buffered(1)
    act = jax.ShapeDtypeStruct((M, width), _BF16)
    return pl.pallas_call(
        functools.partial(_inproj_kernel, q_scale=np.float32(_LOG2E / np.sqrt(HEAD_DIM))),
        grid=(M // tm,),
        in_specs=[pl.BlockSpec((tm, D), row),
                  pl.BlockSpec((1, N_MOD, D), lambda m: (m // tiles_per_seq, 0, 0)),
                  pl.BlockSpec((1, D), const2),
                  pl.BlockSpec(w6.shape, const2, pipeline_mode=resident),
                  pl.BlockSpec((1, HEAD_DIM), const2),
                  pl.BlockSpec((1, HEAD_DIM), const2),
                  pl.BlockSpec((1, width), const2),
                  pl.BlockSpec(w_s.shape, lambda m: (0, 0, 0)),
                  pl.BlockSpec(b_s.shape, const2)],
        out_specs=[pl.BlockSpec((tm, width), row)] * 4 + [pl.BlockSpec((tm, LANES), row)],
        out_shape=[act] * 4 + [jax.ShapeDtypeStruct((M, LANES), _F32)],
        scratch_shapes=[pltpu.VMEM((tm, D), _BF16), pltpu.VMEM((tm, width), _BF16),
                        pltpu.VMEM((tm, width), _F32)],
        compiler_params=_params(("parallel",), _VMEM_MIB["inproj"]),
        name="inproj",
    )(x2, mod3, g_mix, w6, g_q, g_k, g_sgu, w_s, b_s)


def _fcumsum_kernel(f_ref, bf_ref, fcol_ref, frow_ref, *, n_heads):
    S = f_ref.shape[0]
    r = lax.broadcasted_iota(jnp.int32, (CHUNK, CHUNK), 0)
    cidx = lax.broadcasted_iota(jnp.int32, (CHUNK, CHUNK), 1)
    tri = jnp.where(cidx <= r, 1.0, 0.0).astype(_BF16)
    carry = jnp.zeros((1, LANES), _F32)
    for ci in range(S // CHUNK):
        sl = slice(ci * CHUNK, (ci + 1) * CHUNK)
        lf = jax.nn.log_sigmoid(f_ref[sl, :] + bf_ref[...])
        hi = lf.astype(_BF16)
        r1 = lf - hi.astype(_F32)
        mid = r1.astype(_BF16)
        lo = (r1 - mid.astype(_F32)).astype(_BF16)
        cs = (jnp.dot(tri, hi, preferred_element_type=_F32)
              + jnp.dot(tri, mid, preferred_element_type=_F32)
              + jnp.dot(tri, lo, preferred_element_type=_F32))
        F = cs + carry
        F2 = F * _LOG2E
        fcol_ref[sl, :] = F2
        frow_ref[0, :, sl] = F2.T[0:n_heads, :]
        carry = F[CHUNK - 1:CHUNK, :]


def _fcumsum(flog, b_f_pad, *, batch, seq, n_heads):
    return pl.pallas_call(
        functools.partial(_fcumsum_kernel, n_heads=n_heads),
        grid=(batch,),
        in_specs=[pl.BlockSpec((seq, LANES), lambda b: (b, 0)),
                  pl.BlockSpec((1, LANES), lambda b: (0, 0))],
        out_specs=[pl.BlockSpec((seq, LANES), lambda b: (b, 0)),
                   pl.BlockSpec((1, n_heads, seq), lambda b: (b, 0, 0))],
        out_shape=[jax.ShapeDtypeStruct((batch * seq, LANES), _F32),
                   jax.ShapeDtypeStruct((batch, n_heads, seq), _F32)],
        compiler_params=_params(("parallel",), _VMEM_MIB["fcumsum"]),
        name="fcumsum",
    )(flog, b_f_pad)


def _attn_kernel(q_ref, k_ref, v_ref, fcol_ref, frow_ref, w1_ref, w2_ref, w3_ref,
                 o_ref, w1b_ref, w2b_ref, w3b_ref, v2_scr, s_scr, *, tq, hps):
    w1b_ref[...] = w1_ref[...].astype(w1b_ref.dtype)
    w2b_ref[...] = w2_ref[...].astype(w2b_ref.dtype)
    w3b_ref[...] = w3_ref[...].astype(w3b_ref.dtype)

    hp = pl.program_id(1)
    S = q_ref.shape[1]
    n_blk = S // tq
    hsl = [slice(hh * HEAD_DIM, (hh + 1) * HEAD_DIM) for hh in range(hps)]
    den = slice(HEAD_DIM, 2 * HEAD_DIM)
    tiles = [(i, j) for i in range(n_blk) for j in range(i + 1)]

    for hh in range(hps):
        v2_scr[hh, :, 0:HEAD_DIM] = v_ref[0, :, hsl[hh]]
        v2_scr[hh, :, den] = jnp.ones((S, HEAD_DIM), _BF16)

    lane = lax.broadcasted_iota(jnp.int32, (tq, LANES), 1)
    rows = lax.broadcasted_iota(jnp.int32, (tq, tq), 0)
    cols = lax.broadcasted_iota(jnp.int32, (tq, tq), 1)
    causal = cols <= rows

    half = tq // 2

    def parts(i, j):
        if j < i:
            return [(0, tq, 0, tq)]
        return [(0, half, 0, half), (half, tq, 0, tq)]

    def scores(t):
        i, j = tiles[t]
        for hh in range(hps):
            for r0, r1, c0, c1 in parts(i, j):
                s = lax.dot_general(q_ref[0, i * tq + r0:i * tq + r1, hsl[hh]],
                                    k_ref[0, j * tq + c0:j * tq + c1, hsl[hh]],
                                    (((1,), (1,)), ((), ())), preferred_element_type=_F32)
                s_scr[t % 2, hh, r0:r1, c0:c1] = (
                    s - frow_ref[0, 0, hh:hh + 1, j * tq + c0:j * tq + c1])

    scores(0)
    fq = m = acc = None
    for t, (i, j) in enumerate(tiles):
        if t + 1 < len(tiles):
            scores(t + 1)
        if j == 0:
            fc = fcol_ref[0, i * tq:(i + 1) * tq, :]
            fq = [jnp.sum(jnp.where(lane == hp * hps + hh, fc, 0.0), axis=-1, keepdims=True)
                  for hh in range(hps)]
            m = [None] * hps
            acc = [None] * hps
        for hh in range(hps):
            for r0, r1, c0, c1 in parts(i, j):
                a = s_scr[t % 2, hh, r0:r1, c0:c1]
                if j == i:
                    a = jnp.where(causal[r0:r1, c0:c1], a, -jnp.inf)
                fq_r = fq[hh][r0:r1, :]
                m_tile = jnp.max(a, axis=-1, keepdims=True) + fq_r
                m_new = m_tile if j == 0 else jnp.maximum(m[hh][r0:r1, :], m_tile)
                p = jnp.exp2(a - (m_new - fq_r)).astype(_BF16)
                pv = jnp.dot(p, v2_scr[hh, j * tq + c0:j * tq + c1, :],
                             preferred_element_type=_F32)
                acc_new = pv if j == 0 else (
                    jnp.exp2(m[hh][r0:r1, :] - m_new) * acc[hh][r0:r1, :] + pv)
                if j == i:
                    o_ref[0, i * tq + r0:i * tq + r1, hsl[hh]] = (
                        acc_new[:, 0:HEAD_DIM] / acc_new[:, den]).astype(o_ref.dtype)
                else:
                    m[hh], acc[hh] = m_new, acc_new


def _attention(q, k, v, fcol, frow, weights, *, tq=512, hps=2):
    B, S, W = q.shape
    H = W // HEAD_DIM
    n_hp = H // hps
    n_steps = B * n_hp
    heads = lambda b, h: (b, 0, h)
    slab = lambda b, h: (b * n_hp + h, 0)
    w_specs = []
    for w in weights:
        rows = w.shape[0] // n_steps
        assert rows * n_steps == w.shape[0]
        w_specs.append(pl.BlockSpec((rows, w.shape[1]), slab))
    return pl.pallas_call(
        functools.partial(_attn_kernel, tq=tq, hps=hps),
        grid=(B, n_hp),
        in_specs=[pl.BlockSpec((1, S, hps * HEAD_DIM), heads),
                  pl.BlockSpec((1, S, hps * HEAD_DIM), heads),
                  pl.BlockSpec((1, S, hps * HEAD_DIM), heads),
                  pl.BlockSpec((1, S, LANES), lambda b, h: (b, 0, 0)),
                  pl.BlockSpec((1, 1, hps, S), lambda b, h: (b, h, 0, 0))] + w_specs,
        out_specs=[pl.BlockSpec((1, S, hps * HEAD_DIM), heads)] + w_specs,
        out_shape=[jax.ShapeDtypeStruct((B, S, W), _BF16)]
                  + [jax.ShapeDtypeStruct(w.shape, _BF16) for w in weights],
        scratch_shapes=[pltpu.VMEM((hps, S, 2 * HEAD_DIM), _BF16),
                        pltpu.VMEM((2, hps, tq, tq), _F32)],
        compiler_params=_params(("parallel", "parallel"), _VMEM_MIB["attention"]),
        name="attention",
    )(q, k, v, fcol, frow.reshape(B, n_hp, hps, S), *weights)


def _outproj_kernel(a_ref, s_ref, w_ref, x_ref, mod_ref, o_ref):
    wa = a_ref.shape[1]
    mix = (jnp.dot(a_ref[...], w_ref[0:wa, :], preferred_element_type=_F32)
           + jnp.dot(s_ref[...], w_ref[wa:, :], preferred_element_type=_F32))
    o_ref[...] = x_ref[...] + mod_ref[0, _GATE1:_GATE1 + 1, :] * mix


def _outproj(attn, sgu, w_out, x2, mod3, *, seq, tm=512):
    M, D = x2.shape
    wa, ws = attn.shape[1], sgu.shape[1]
    tiles_per_seq = seq // tm
    row = lambda m: (m, 0)
    return pl.pallas_call(
        _outproj_kernel,
        grid=(M // tm,),
        in_specs=[pl.BlockSpec((tm, wa), row),
                  pl.BlockSpec((tm, ws), row),
                  pl.BlockSpec((wa + ws, D), lambda m: (0, 0)),
                  pl.BlockSpec((tm, D), row),
                  pl.BlockSpec((1, N_MOD, D), lambda m: (m // tiles_per_seq, 0, 0))],
        out_specs=pl.BlockSpec((tm, D), row),
        out_shape=jax.ShapeDtypeStruct((M, D), _F32),
        compiler_params=_params(("parallel",), _VMEM_MIB["outproj"]),
        name="outproj",
    )(attn, sgu, w_out, x2, mod3)


def _ffn_kernel(x_ref, mod_ref, g_ref, w1_ref, w2_ref, o_ref, h_scr):
    f = pl.program_id(1)

    @pl.when(f == 0)
    def _():
        _norm_modulate(x_ref, g_ref[...], mod_ref[0, _SHIFT2:_SHIFT2 + 1, :],
                       mod_ref[0, _SCALE2:_SCALE2 + 1, :], h_scr, copy_ref=o_ref)

    hid = jnp.dot(h_scr[...], w1_ref[...], preferred_element_type=_F32)
    hid = jnp.square(jnp.maximum(hid, 0.0)).astype(_BF16)
    part = jnp.dot(hid, w2_ref[...], preferred_element_type=_F32)
    o_ref[...] += mod_ref[0, _GATE2:_GATE2 + 1, :] * part


def _ffn(x2, mod3, g_ffn, w1, w2, *, seq, tm=512, tf=1024):
    M, D = x2.shape
    F = w1.shape[1]
    tiles_per_seq = seq // tm
    row = lambda m, f: (m, 0)
    return pl.pallas_call(
        _ffn_kernel,
        grid=(M // tm, F // tf),
        in_specs=[pl.BlockSpec((tm, D), row),
                  pl.BlockSpec((1, N_MOD, D), lambda m, f: (m // tiles_per_seq, 0, 0)),
                  pl.BlockSpec((1, D), lambda m, f: (0, 0)),
                  pl.BlockSpec((D, tf), lambda m, f: (0, f)),
                  pl.BlockSpec((tf, D), lambda m, f: (f, 0))],
        out_specs=pl.BlockSpec((tm, D), row),
        out_shape=jax.ShapeDtypeStruct((M, D), _F32),
        scratch_shapes=[pltpu.VMEM((tm, D), _BF16)],
        compiler_params=_params(("parallel", "arbitrary"), _VMEM_MIB["ffn"]),
        name="ffn",
    )(x2, mod3, g_ffn, w1, w2)


def kernel(x, c, w_ada, b_ada, g_mix, w_in, b_f, g_q, g_k, g_sgu, w_s, b_s, w_out, g_ffn,
           w_ff1, w_ff2):
    B, S, D = x.shape
    depth = w_ada.shape[0]
    H = b_f.shape[1]
    G = w_s.shape[1]
    attn_w = H * HEAD_DIM
    gmlp_w = G * HEAD_DIM
    x2 = x.reshape(B * S, D)
    for l in range(depth):
        o_f = 3 * attn_w
        o_u = o_f + H
        assert attn_w == gmlp_w and w_in.shape[2] == o_u + 2 * gmlp_w
        mod, w6 = _adaln(c, w_ada[l], b_ada[l], w_in[l].T, o_f=o_f, o_u=o_u)
        mod3 = mod.reshape(B, N_MOD, D)
        b_f_pad = jnp.pad(b_f[l], (0, LANES - H)).reshape(1, LANES)

        q, k, v, sgu, flog = _inproj(
            x2, mod3, g_mix[l].reshape(1, D), w6, g_q[l].reshape(1, HEAD_DIM),
            g_k[l].reshape(1, HEAD_DIM), g_sgu[l].reshape(1, gmlp_w), w_s[l], b_s[l], seq=S)

        fcol, frow = _fcumsum(flog, b_f_pad, batch=B, seq=S, n_heads=H)
        attn, w1b, w2b, wob = _attention(
            q.reshape(B, S, attn_w), k.reshape(B, S, attn_w), v.reshape(B, S, attn_w),
            fcol.reshape(B, S, LANES), frow, (w_ff1[l], w_ff2[l], w_out[l]))
        x2 = _outproj(attn.reshape(B * S, attn_w), sgu, wob, x2, mod3, seq=S)
        x2 = _ffn(x2, mod3, g_ffn[l].reshape(1, D), w1b, w2b, seq=S)
    return x2.reshape(B, S, D)
```

```python
import functools

import jax
import jax.numpy as jnp
import numpy as np
from jax import lax
from jax.experimental import pallas as pl
from jax.experimental.pallas import tpu as pltpu

_F32 = jnp.float32
_BF16 = jnp.bfloat16

HEAD_DIM = 128
CHUNK = 128
N_MOD = 6
EPS = 1e-6
LANES = 128
_LOG2E = np.float32(np.log2(np.e))

_SHIFT1, _SCALE1, _GATE1, _SHIFT2, _SCALE2, _GATE2 = range(6)

_MIB = 1024 * 1024
_VMEM_MIB = {"adaln": 40, "inproj": 56, "fcumsum": 32, "attention": 48, "outproj": 48, "ffn": 52}


def _params(semantics, vmem_mib):
    return pltpu.CompilerParams(dimension_semantics=semantics,
                                vmem_limit_bytes=vmem_mib * _MIB)


def _adaln_kernel(c_ref, w_ref, b_ref, win_ref, o_ref, w6_ref, *, o_f, o_u):
    c = c_ref[...]
    c_act = (c * jax.nn.sigmoid(c)).astype(_BF16)
    w = w_ref[...].astype(_BF16)
    o_ref[...] = jnp.dot(c_act, w, preferred_element_type=_F32) + b_ref[...]

    width = (win_ref.shape[0] - o_u) // 2
    w6_ref[0:width, :] = win_ref[o_u + width:, :].astype(w6_ref.dtype)
    w6_ref[width:2 * width, :] = win_ref[o_u:o_u + width, :].astype(w6_ref.dtype)
    w6_ref[2 * width:5 * width, :] = win_ref[0:o_f, :].astype(w6_ref.dtype)
    row = lax.broadcasted_iota(jnp.int32, (LANES, w6_ref.shape[1]), 0)
    w6_ref[5 * width:, :] = jnp.where(row < o_u - o_f, win_ref[o_f:o_f + LANES, :],
                                      0.0).astype(w6_ref.dtype)


def _adaln(c, w_ada, b_ada, w_in_t, *, o_f, o_u, n_steps=16):
    B, D = c.shape
    N = w_ada.shape[1]
    cols = w_in_t.shape[0]
    tn, td = N // n_steps, D // n_steps
    assert tn * n_steps == N and tn % LANES == 0 and td * n_steps == D and td % LANES == 0
    n_rows = cols - (o_u - o_f) + LANES
    dcol = lambda n: (0, n)
    return pl.pallas_call(
        functools.partial(_adaln_kernel, o_f=o_f, o_u=o_u),
        grid=(n_steps,),
        in_specs=[pl.BlockSpec((B, D), lambda n: (0, 0)),
                  pl.BlockSpec((D, tn), dcol),
                  pl.BlockSpec((1, tn), dcol),
                  pl.BlockSpec((cols, td), dcol)],
        out_specs=[pl.BlockSpec((B, tn), dcol),
                   pl.BlockSpec((n_rows, td), dcol)],
        out_shape=[jax.ShapeDtypeStruct((B, N), _F32),
                   jax.ShapeDtypeStruct((n_rows, D), _BF16)],
        compiler_params=_params(("parallel",), _VMEM_MIB["adaln"]),
        name="adaln",
    )(c, w_ada, b_ada.reshape(1, N), w_in_t)


_NORM_ROWS = 16


def _norm_modulate(x_ref, g, shift, scale, h_ref, copy_ref=None):
    one_plus_scale = 1.0 + scale
    for r in range(0, x_ref.shape[0], _NORM_ROWS):
        rows = slice(r, r + _NORM_ROWS)
        x = x_ref[rows, :]
        ms = jnp.mean(x * x, axis=-1, keepdims=True)
        y = x * lax.rsqrt(ms + EPS) * g
        h_ref[rows, :] = (y * one_plus_scale + shift).astype(h_ref.dtype)
        if copy_ref is not None:
            copy_ref[rows, :] = x


def _inproj_kernel(x_ref, mod_ref, gmix_ref, w_ref, gq_ref, gk_ref, gsgu_ref, ws_ref,
                   bs_ref, q_ref, k_ref, v_ref, sgu_ref, f_ref, h_scr, vn_scr, z_scr, *, q_scale):
    width = q_ref.shape[-1]
    n_heads = width // HEAD_DIM
    sec_vg, sec_u, sec_q, sec_k, sec_v = range(5)

    _norm_modulate(x_ref, gmix_ref[...], mod_ref[0, _SHIFT1:_SHIFT1 + 1, :],
                   mod_ref[0, _SCALE1:_SCALE1 + 1, :], h_scr)
    proj = lax.dot_general(h_scr[...], w_ref[...], (((1,), (1,)), ((), ())),
                           preferred_element_type=_F32)
    f_ref[...] = proj[:, 5 * width:]

    def section(idx):
        return proj[:, idx * width:(idx + 1) * width]

    def head_rms(acc, o_ref, g, scale):
        for hh in range(n_heads):
            sl = slice(hh * HEAD_DIM, (hh + 1) * HEAD_DIM)
            blk = acc[:, sl]
            ms = jnp.mean(blk * blk, axis=-1, keepdims=True)
            y = blk * lax.rsqrt(ms + EPS) * g
            if scale is not None:
                y = y * scale
            o_ref[:, sl] = y.astype(o_ref.dtype)

    acc = section(sec_vg)
    for gg in range(n_heads):
        sl = slice(gg * HEAD_DIM, (gg + 1) * HEAD_DIM)
        blk = jax.nn.gelu(acc[:, sl])
        mu = jnp.mean(blk, axis=-1, keepdims=True)
        xc = blk - mu
        var = jnp.mean(xc * xc, axis=-1, keepdims=True)
        y = xc * lax.rsqrt(var + EPS) * gsgu_ref[:, sl]
        vn_scr[:, sl] = y.astype(vn_scr.dtype)

    r = lax.broadcasted_iota(jnp.int32, (CHUNK, CHUNK), 0)
    cidx = lax.broadcasted_iota(jnp.int32, (CHUNK, CHUNK), 1)
    lower = cidx <= r
    for gg in range(n_heads):
        sl = slice(gg * HEAD_DIM, (gg + 1) * HEAD_DIM)
        w = jnp.where(lower, ws_ref[gg], 0.0).astype(_BF16)
        bcol = jnp.broadcast_to(bs_ref[gg:gg + 1, :], (CHUNK, CHUNK)).T
        for ci in range(x_ref.shape[0] // CHUNK):
            rs = slice(ci * CHUNK, (ci + 1) * CHUNK)
            z_scr[rs, sl] = jnp.dot(w, vn_scr[rs, sl], preferred_element_type=_F32) + bcol

    head_rms(section(sec_q), q_ref, gq_ref[...], q_scale)
    head_rms(section(sec_k), k_ref, gk_ref[...], None)
    sgu_ref[...] = (jax.nn.gelu(section(sec_u)) * z_scr[...]).astype(sgu_ref.dtype)
    v_ref[...] = section(sec_v).astype(v_ref.dtype)


def _inproj(x2, mod3, g_mix, w6, g_q, g_k, g_sgu, w_s, b_s, *, seq, tm=512):
    M, D = x2.shape
    n_sec = 5
    width = (w6.shape[0] - LANES) // n_sec
    tiles_per_seq = seq // tm
    row = lambda m: (m, 0)
    const2 = lambda m: (0, 0)
    resident = pl.Buffered(1)
    act = jax.ShapeDtypeStruct((M, width), _BF16)
    return pl.pallas_call(
        functools.partial(_inproj_kernel, q_scale=np.float32(_LOG2E / np.sqrt(HEAD_DIM))),
        grid=(M // tm,),
        in_specs=[pl.BlockSpec((tm, D), row),
                  pl.BlockSpec((1, N_MOD, D), lambda m: (m // tiles_per_seq, 0, 0)),
                  pl.BlockSpec((1, D), const2),
                  pl.BlockSpec(w6.shape, const2, pipeline_mode=resident),
                  pl.BlockSpec((1, HEAD_DIM), const2),
                  pl.BlockSpec((1, HEAD_DIM), const2),
                  pl.BlockSpec((1, width), const2),
                  pl.BlockSpec(w_s.shape, lambda m: (0, 0, 0)),
                  pl.BlockSpec(b_s.shape, const2)],
        out_specs=[pl.BlockSpec((tm, width), row)] * 4 + [pl.BlockSpec((tm, LANES), row)],
        out_shape=[act] * 4 + [jax.ShapeDtypeStruct((M, LANES), _F32)],
        scratch_shapes=[pltpu.VMEM((tm, D), _BF16), pltpu.VMEM((tm, width), _BF16),
                        pltpu.VMEM((tm, width), _F32)],
        compiler_params=_params(("parallel",), _VMEM_MIB["inproj"]),
        name="inproj",
    )(x2, mod3, g_mix, w6, g_q, g_k, g_sgu, w_s, b_s)


def _fcumsum_kernel(f_ref, bf_ref, fcol_ref, frow_ref, *, n_heads):
    S = f_ref.shape[0]
    r = lax.broadcasted_iota(jnp.int32, (CHUNK, CHUNK), 0)
    cidx = lax.broadcasted_iota(jnp.int32, (CHUNK, CHUNK), 1)
    tri = jnp.where(cidx <= r, 1.0, 0.0).astype(_BF16)
    carry = jnp.zeros((1, LANES), _F32)
    for ci in range(S // CHUNK):
        sl = slice(ci * CHUNK, (ci + 1) * CHUNK)
        lf = jax.nn.log_sigmoid(f_ref[sl, :] + bf_ref[...])
        hi = lf.astype(_BF16)
        r1 = lf - hi.astype(_F32)
        mid = r1.astype(_BF16)
        lo = (r1 - mid.astype(_F32)).astype(_BF16)
        cs = (jnp.dot(tri, hi, preferred_element_type=_F32)
              + jnp.dot(tri, mid, preferred_element_type=_F32)
              + jnp.dot(tri, lo, preferred_element_type=_F32))
        F = cs + carry
        F2 = F * _LOG2E
        fcol_ref[sl, :] = F2
        frow_ref[0, :, sl] = F2.T[0:n_heads, :]
        carry = F[CHUNK - 1:CHUNK, :]


def _fcumsum(flog, b_f_pad, *, batch, seq, n_heads):
    return pl.pallas_call(
        functools.partial(_fcumsum_kernel, n_heads=n_heads),
        grid=(batch,),
        in_specs=[pl.BlockSpec((seq, LANES), lambda b: (b, 0)),
                  pl.BlockSpec((1, LANES), lambda b: (0, 0))],
        out_specs=[pl.BlockSpec((seq, LANES), lambda b: (b, 0)),
                   pl.BlockSpec((1, n_heads, seq), lambda b: (b, 0, 0))],
        out_shape=[jax.ShapeDtypeStruct((batch * seq, LANES), _F32),
                   jax.ShapeDtypeStruct((batch, n_heads, seq), _F32)],
        compiler_params=_params(("parallel",), _VMEM_MIB["fcumsum"]),
        name="fcumsum",
    )(flog, b_f_pad)


def _attn_kernel(q_ref, k_ref, v_ref, fcol_ref, frow_ref, w1_ref, w2_ref, w3_ref,
                 o_ref, w1b_ref, w2b_ref, w3b_ref, v2_scr, s_scr, *, tq, tk, hps):
    w1b_ref[...] = w1_ref[...].astype(w1b_ref.dtype)
    w2b_ref[...] = w2_ref[...].astype(w2b_ref.dtype)
    w3b_ref[...] = w3_ref[...].astype(w3b_ref.dtype)

    hp = pl.program_id(1)
    S = q_ref.shape[1]
    n_blk = S // tq
    hsl = [slice(hh * HEAD_DIM, (hh + 1) * HEAD_DIM) for hh in range(hps)]
    den = slice(HEAD_DIM, 2 * HEAD_DIM)
    tiles = [(i, k0, min(k0 + tk, (i + 1) * tq))
             for i in range(n_blk) for k0 in range(0, (i + 1) * tq, tk)]

    for hh in range(hps):
        v2_scr[hh, :, 0:HEAD_DIM] = v_ref[0, :, hsl[hh]]
        v2_scr[hh, :, den] = jnp.ones((S, HEAD_DIM), _BF16)

    lane = lax.broadcasted_iota(jnp.int32, (tq, LANES), 1)
    rows = lax.broadcasted_iota(jnp.int32, (tq, tq), 0)
    cols = lax.broadcasted_iota(jnp.int32, (tq, tq), 1)
    causal = cols <= rows

    half = tq // 2

    def parts(i, k0, k1):
        if k1 <= i * tq:
            return [(0, tq, k0, k1)]
        return [(0, half, k0, i * tq + half), (half, tq, k0, k1)]

    def scores(t):
        i, k0, k1 = tiles[t]
        for hh in range(hps):
            for r0, r1, c0, c1 in parts(i, k0, k1):
                s = lax.dot_general(q_ref[0, i * tq + r0:i * tq + r1, hsl[hh]],
                                    k_ref[0, c0:c1, hsl[hh]],
                                    (((1,), (1,)), ((), ())), preferred_element_type=_F32)
                s_scr[t % 2, hh, r0:r1, 0:c1 - c0] = s - frow_ref[0, 0, hh:hh + 1, c0:c1]

    scores(0)
    fq = m = acc = None
    for t, (i, k0, k1) in enumerate(tiles):
        if t + 1 < len(tiles):
            scores(t + 1)
        first, last = k0 == 0, k1 == (i + 1) * tq
        if first:
            fc = fcol_ref[0, i * tq:(i + 1) * tq, :]
            fq = [jnp.sum(jnp.where(lane == hp * hps + hh, fc, 0.0), axis=-1, keepdims=True)
                  for hh in range(hps)]
            m = [None] * hps
            acc = [None] * hps
        for hh in range(hps):
            new = []
            for r0, r1, c0, c1 in parts(i, k0, k1):
                a = s_scr[t % 2, hh, r0:r1, 0:c1 - c0]
                if last:
                    d0 = i * tq - c0
                    diag = jnp.where(causal[r0:r1, 0:c1 - i * tq], a[:, d0:], -jnp.inf)
                    a = diag if d0 == 0 else jnp.concatenate([a[:, 0:d0], diag], axis=1)
                fq_r = fq[hh][r0:r1, :]
                m_tile = jnp.max(a, axis=-1, keepdims=True) + fq_r
                m_new = m_tile if first else jnp.maximum(m[hh][r0:r1, :], m_tile)
                p = jnp.exp2(a - (m_new - fq_r)).astype(_BF16)
                pv = jnp.dot(p, v2_scr[hh, c0:c1, :], preferred_element_type=_F32)
                acc_new = pv if first else (
                    jnp.exp2(m[hh][r0:r1, :] - m_new) * acc[hh][r0:r1, :] + pv)
                if last:
                    o_ref[0, i * tq + r0:i * tq + r1, hsl[hh]] = (
                        acc_new[:, 0:HEAD_DIM] / acc_new[:, den]).astype(o_ref.dtype)
                new.append((m_new, acc_new))
            if not last:
                m[hh], acc[hh] = new[0]


def _attention(q, k, v, fcol, frow, weights, *, tq=512, tk=2048, hps=2):
    B, S, W = q.shape
    H = W // HEAD_DIM
    n_hp = H // hps
    n_steps = B * n_hp
    heads = lambda b, h: (b, 0, h)
    slab = lambda b, h: (b * n_hp + h, 0)
    w_specs = []
    for w in weights:
        rows = w.shape[0] // n_steps
        assert rows * n_steps == w.shape[0]
        w_specs.append(pl.BlockSpec((rows, w.shape[1]), slab))
    return pl.pallas_call(
        functools.partial(_attn_kernel, tq=tq, tk=tk, hps=hps),
        grid=(B, n_hp),
        in_specs=[pl.BlockSpec((1, S, hps * HEAD_DIM), heads),
                  pl.BlockSpec((1, S, hps * HEAD_DIM), heads),
                  pl.BlockSpec((1, S, hps * HEAD_DIM), heads),
                  pl.BlockSpec((1, S, LANES), lambda b, h: (b, 0, 0)),
                  pl.BlockSpec((1, 1, hps, S), lambda b, h: (b, h, 0, 0))] + w_specs,
        out_specs=[pl.BlockSpec((1, S, hps * HEAD_DIM), heads)] + w_specs,
        out_shape=[jax.ShapeDtypeStruct((B, S, W), _BF16)]
                  + [jax.ShapeDtypeStruct(w.shape, _BF16) for w in weights],
        scratch_shapes=[pltpu.VMEM((hps, S, 2 * HEAD_DIM), _BF16),
                        pltpu.VMEM((2, hps, tq, tk), _F32)],
        compiler_params=_params(("parallel", "parallel"), _VMEM_MIB["attention"]),
        name="attention",
    )(q, k, v, fcol, frow.reshape(B, n_hp, hps, S), *weights)


def _outproj_kernel(a_ref, s_ref, w_ref, x_ref, mod_ref, o_ref):
    wa = a_ref.shape[1]
    mix = (jnp.dot(a_ref[...], w_ref[0:wa, :], preferred_element_type=_F32)
           + jnp.dot(s_ref[...], w_ref[wa:, :], preferred_element_type=_F32))
    o_ref[...] = x_ref[...] + mod_ref[0, _GATE1:_GATE1 + 1, :] * mix


def _outproj(attn, sgu, w_out, x2, mod3, *, seq, tm=512):
    M, D = x2.shape
    wa, ws = attn.shape[1], sgu.shape[1]
    tiles_per_seq = seq // tm
    row = lambda m: (m, 0)
    return pl.pallas_call(
        _outproj_kernel,
        grid=(M // tm,),
        in_specs=[pl.BlockSpec((tm, wa), row),
                  pl.BlockSpec((tm, ws), row),
                  pl.BlockSpec((wa + ws, D), lambda m: (0, 0)),
                  pl.BlockSpec((tm, D), row),
                  pl.BlockSpec((1, N_MOD, D), lambda m: (m // tiles_per_seq, 0, 0))],
        out_specs=pl.BlockSpec((tm, D), row),
        out_shape=jax.ShapeDtypeStruct((M, D), _F32),
        compiler_params=_params(("parallel",), _VMEM_MIB["outproj"]),
        name="outproj",
    )(attn, sgu, w_out, x2, mod3)


def _ffn_kernel(x_ref, mod_ref, g_ref, w1_ref, w2_ref, o_ref, h_scr):
    f = pl.program_id(1)

    @pl.when(f == 0)
    def _():
        _norm_modulate(x_ref, g_ref[...], mod_ref[0, _SHIFT2:_SHIFT2 + 1, :],
                       mod_ref[0, _SCALE2:_SCALE2 + 1, :], h_scr, copy_ref=o_ref)

    hid = jnp.dot(h_scr[...], w1_ref[...], preferred_element_type=_F32)
    hid = jnp.square(jnp.maximum(hid, 0.0)).astype(_BF16)
    part = jnp.dot(hid, w2_ref[...], preferred_element_type=_F32)
    o_ref[...] += mod_ref[0, _GATE2:_GATE2 + 1, :] * part


def _ffn(x2, mod3, g_ffn, w1, w2, *, seq, tm=512, tf=1024):
    M, D = x2.shape
    F = w1.shape[1]
    tiles_per_seq = seq // tm
    row = lambda m, f: (m, 0)
    return pl.pallas_call(
        _ffn_kernel,
        grid=(M // tm, F // tf),
        in_specs=[pl.BlockSpec((tm, D), row),
                  pl.BlockSpec((1, N_MOD, D), lambda m, f: (m // tiles_per_seq, 0, 0)),
                  pl.BlockSpec((1, D), lambda m, f: (0, 0)),
                  pl.BlockSpec((D, tf), lambda m, f: (0, f)),
                  pl.BlockSpec((tf, D), lambda m, f: (f, 0))],
        out_specs=pl.BlockSpec((tm, D), row),
        out_shape=jax.ShapeDtypeStruct((M, D), _F32),
        scratch_shapes=[pltpu.VMEM((tm, D), _BF16)],
        compiler_params=_params(("parallel", "arbitrary"), _VMEM_MIB["ffn"]),
        name="ffn",
    )(x2, mod3, g_ffn, w1, w2)


def kernel(x, c, w_ada, b_ada, g_mix, w_in, b_f, g_q, g_k, g_sgu, w_s, b_s, w_out, g_ffn,
           w_ff1, w_ff2):
    B, S, D = x.shape
    depth = w_ada.shape[0]
    H = b_f.shape[1]
    G = w_s.shape[1]
    attn_w = H * HEAD_DIM
    gmlp_w = G * HEAD_DIM
    x2 = x.reshape(B * S, D)
    for l in range(depth):
        o_f = 3 * attn_w
        o_u = o_f + H
        assert attn_w == gmlp_w and w_in.shape[2] == o_u + 2 * gmlp_w
        mod, w6 = _adaln(c, w_ada[l], b_ada[l], w_in[l].T, o_f=o_f, o_u=o_u)
        mod3 = mod.reshape(B, N_MOD, D)
        b_f_pad = jnp.pad(b_f[l], (0, LANES - H)).reshape(1, LANES)

        q, k, v, sgu, flog = _inproj(
            x2, mod3, g_mix[l].reshape(1, D), w6, g_q[l].reshape(1, HEAD_DIM),
            g_k[l].reshape(1, HEAD_DIM), g_sgu[l].reshape(1, gmlp_w), w_s[l], b_s[l], seq=S)

        fcol, frow = _fcumsum(flog, b_f_pad, batch=B, seq=S, n_heads=H)
        attn, w1b, w2b, wob = _attention(
            q.reshape(B, S, attn_w), k.reshape(B, S, attn_w), v.reshape(B, S, attn_w),
            fcol.reshape(B, S, LANES), frow, (w_ff1[l], w_ff2[l], w_out[l]))
        x2 = _outproj(attn.reshape(B * S, attn_w), sgu, wob, x2, mod3, seq=S)
        x2 = _ffn(x2, mod3, g_ffn[l].reshape(1, D), w1b, w2b, seq=S)
    return x2.reshape(B, S, D)
```

```python
import functools

import jax
import jax.numpy as jnp
import numpy as np
from jax import lax
from jax.experimental import pallas as pl
from jax.experimental.pallas import tpu as pltpu

_F32 = jnp.float32
_BF16 = jnp.bfloat16

HEAD_DIM = 128
CHUNK = 128
N_MOD = 6
EPS = 1e-6
LANES = 128
_LOG2E = np.float32(np.log2(np.e))

_SHIFT1, _SCALE1, _GATE1, _SHIFT2, _SCALE2, _GATE2 = range(6)

_MIB = 1024 * 1024
_VMEM_MIB = {"adaln": 40, "inproj": 56, "fcumsum": 32, "attention": 48, "outproj": 48, "ffn": 52}


def _params(semantics, vmem_mib):
    return pltpu.CompilerParams(dimension_semantics=semantics,
                                vmem_limit_bytes=vmem_mib * _MIB)


def _adaln_kernel(c_ref, w_ref, b_ref, win_ref, o_ref, w6_ref, *, o_f, o_u):
    c = c_ref[...]
    c_act = (c * jax.nn.sigmoid(c)).astype(_BF16)
    w = w_ref[...].astype(_BF16)
    o_ref[...] = jnp.dot(c_act, w, preferred_element_type=_F32) + b_ref[...]

    width = (win_ref.shape[0] - o_u) // 2
    w6_ref[0:width, :] = win_ref[o_u + width:, :].astype(w6_ref.dtype)
    w6_ref[width:2 * width, :] = win_ref[o_u:o_u + width, :].astype(w6_ref.dtype)
    w6_ref[2 * width:5 * width, :] = win_ref[0:o_f, :].astype(w6_ref.dtype)
    row = lax.broadcasted_iota(jnp.int32, (LANES, w6_ref.shape[1]), 0)
    w6_ref[5 * width:, :] = jnp.where(row < o_u - o_f, win_ref[o_f:o_f + LANES, :],
                                      0.0).astype(w6_ref.dtype)


def _adaln(c, w_ada, b_ada, w_in_t, *, o_f, o_u, n_steps=16):
    B, D = c.shape
    N = w_ada.shape[1]
    cols = w_in_t.shape[0]
    tn, td = N // n_steps, D // n_steps
    assert tn * n_steps == N and tn % LANES == 0 and td * n_steps == D and td % LANES == 0
    n_rows = cols - (o_u - o_f) + LANES
    dcol = lambda n: (0, n)
    return pl.pallas_call(
        functools.partial(_adaln_kernel, o_f=o_f, o_u=o_u),
        grid=(n_steps,),
        in_specs=[pl.BlockSpec((B, D), lambda n: (0, 0)),
                  pl.BlockSpec((D, tn), dcol),
                  pl.BlockSpec((1, tn), dcol),
                  pl.BlockSpec((cols, td), dcol)],
        out_specs=[pl.BlockSpec((B, tn), dcol),
                   pl.BlockSpec((n_rows, td), dcol)],
        out_shape=[jax.ShapeDtypeStruct((B, N), _F32),
                   jax.ShapeDtypeStruct((n_rows, D), _BF16)],
        compiler_params=_params(("parallel",), _VMEM_MIB["adaln"]),
        name="adaln",
    )(c, w_ada, b_ada.reshape(1, N), w_in_t)


_NORM_ROWS = 16


def _norm_modulate(x_ref, g, shift, scale, h_ref, copy_ref=None):
    one_plus_scale = 1.0 + scale
    for r in range(0, x_ref.shape[0], _NORM_ROWS):
        rows = slice(r, r + _NORM_ROWS)
        x = x_ref[rows, :]
        ms = jnp.mean(x * x, axis=-1, keepdims=True)
        y = x * lax.rsqrt(ms + EPS) * g
        h_ref[rows, :] = (y * one_plus_scale + shift).astype(h_ref.dtype)
        if copy_ref is not None:
            copy_ref[rows, :] = x


def _inproj_kernel(x_ref, mod_ref, gmix_ref, w_ref, gq_ref, gk_ref, gsgu_ref, ws_ref,
                   bs_ref, q_ref, k_ref, v_ref, sgu_ref, f_ref, h_scr, vn_scr, z_scr, *, q_scale):
    width = q_ref.shape[-1]
    n_heads = width // HEAD_DIM
    sec_vg, sec_u, sec_q, sec_k, sec_v = range(5)

    _norm_modulate(x_ref, gmix_ref[...], mod_ref[0, _SHIFT1:_SHIFT1 + 1, :],
                   mod_ref[0, _SCALE1:_SCALE1 + 1, :], h_scr)
    proj = lax.dot_general(h_scr[...], w_ref[...], (((1,), (1,)), ((), ())),
                           preferred_element_type=_F32)
    f_ref[...] = proj[:, 5 * width:]

    def section(idx):
        return proj[:, idx * width:(idx + 1) * width]

    def head_rms(acc, o_ref, g, scale):
        for hh in range(n_heads):
            sl = slice(hh * HEAD_DIM, (hh + 1) * HEAD_DIM)
            blk = acc[:, sl]
            ms = jnp.mean(blk * blk, axis=-1, keepdims=True)
            y = blk * lax.rsqrt(ms + EPS) * g
            if scale is not None:
                y = y * scale
            o_ref[:, sl] = y.astype(o_ref.dtype)

    acc = section(sec_vg)
    for gg in range(n_heads):
        sl = slice(gg * HEAD_DIM, (gg + 1) * HEAD_DIM)
        blk = jax.nn.gelu(acc[:, sl])
        mu = jnp.mean(blk, axis=-1, keepdims=True)
        xc = blk - mu
        var = jnp.mean(xc * xc, axis=-1, keepdims=True)
        y = xc * lax.rsqrt(var + EPS) * gsgu_ref[:, sl]
        vn_scr[:, sl] = y.astype(vn_scr.dtype)

    r = lax.broadcasted_iota(jnp.int32, (CHUNK, CHUNK), 0)
    cidx = lax.broadcasted_iota(jnp.int32, (CHUNK, CHUNK), 1)
    lower = cidx <= r
    for gg in range(n_heads):
        sl = slice(gg * HEAD_DIM, (gg + 1) * HEAD_DIM)
        w = jnp.where(lower, ws_ref[gg], 0.0).astype(_BF16)
        bcol = jnp.broadcast_to(bs_ref[gg:gg + 1, :], (CHUNK, CHUNK)).T
        for ci in range(x_ref.shape[0] // CHUNK):
            rs = slice(ci * CHUNK, (ci + 1) * CHUNK)
            z_scr[rs, sl] = jnp.dot(w, vn_scr[rs, sl], preferred_element_type=_F32) + bcol

    head_rms(section(sec_q), q_ref, gq_ref[...], q_scale)
    head_rms(section(sec_k), k_ref, gk_ref[...], None)
    sgu_ref[...] = (jax.nn.gelu(section(sec_u)) * z_scr[...]).astype(sgu_ref.dtype)
    v_ref[...] = section(sec_v).astype(v_ref.dtype)


def _inproj(x2, mod3, g_mix, w6, g_q, g_k, g_sgu, w_s, b_s, *, seq, tm=512):
    M, D = x2.shape
    n_sec = 5
    width = (w6.shape[0] - LANES) // n_sec
    tiles_per_seq = seq // tm
    row = lambda m: (m, 0)
    const2 = lambda m: (0, 0)
    resident = pl.Buffered(1)
    act = jax.ShapeDtypeStruct((M, width), _BF16)
    return pl.pallas_call(
        functools.partial(_inproj_kernel, q_scale=np.float32(_LOG2E / np.sqrt(HEAD_DIM))),
        grid=(M // tm,),
        in_specs=[pl.BlockSpec((tm, D), row),
                  pl.BlockSpec((1, N_MOD, D), lambda m: (m // tiles_per_seq, 0, 0)),
                  pl.BlockSpec((1, D), const2),
                  pl.BlockSpec(w6.shape, const2, pipeline_mode=resident),
                  pl.BlockSpec((1, HEAD_DIM), const2),
                  pl.BlockSpec((1, HEAD_DIM), const2),
                  pl.BlockSpec((1, width), const2),
                  pl.BlockSpec(w_s.shape, lambda m: (0, 0, 0)),
                  pl.BlockSpec(b_s.shape, const2)],
        out_specs=[pl.BlockSpec((tm, width), row)] * 4 + [pl.BlockSpec((tm, LANES), row)],
        out_shape=[act] * 4 + [jax.ShapeDtypeStruct((M, LANES), _F32)],
        scratch_shapes=[pltpu.VMEM((tm, D), _BF16), pltpu.VMEM((tm, width), _BF16),
                        pltpu.VMEM((tm, width), _F32)],
        compiler_params=_params(("parallel",), _VMEM_MIB["inproj"]),
        name="inproj",
    )(x2, mod3, g_mix, w6, g_q, g_k, g_sgu, w_s, b_s)


def _fcumsum_kernel(f_ref, bf_ref, fcol_ref, frow_ref, *, n_heads):
    S = f_ref.shape[0]
    r = lax.broadcasted_iota(jnp.int32, (CHUNK, CHUNK), 0)
    cidx = lax.broadcasted_iota(jnp.int32, (CHUNK, CHUNK), 1)
    tri = jnp.where(cidx <= r, 1.0, 0.0).astype(_BF16)
    carry = jnp.zeros((1, LANES), _F32)
    for ci in range(S // CHUNK):
        sl = slice(ci * CHUNK, (ci + 1) * CHUNK)
        lf = jax.nn.log_sigmoid(f_ref[sl, :] + bf_ref[...])
        hi = lf.astype(_BF16)
        r1 = lf - hi.astype(_F32)
        mid = r1.astype(_BF16)
        lo = (r1 - mid.astype(_F32)).astype(_BF16)
        cs = (jnp.dot(tri, hi, preferred_element_type=_F32)
              + jnp.dot(tri, mid, preferred_element_type=_F32)
              + jnp.dot(tri, lo, preferred_element_type=_F32))
        F = cs + carry
        F2 = F * _LOG2E
        fcol_ref[sl, :] = F2
        frow_ref[0, :, sl] = F2.T[0:n_heads, :]
        carry = F[CHUNK - 1:CHUNK, :]


def _fcumsum(flog, b_f_pad, *, batch, seq, n_heads):
    return pl.pallas_call(
        functools.partial(_fcumsum_kernel, n_heads=n_heads),
        grid=(batch,),
        in_specs=[pl.BlockSpec((seq, LANES), lambda b: (b, 0)),
                  pl.BlockSpec((1, LANES), lambda b: (0, 0))],
        out_specs=[pl.BlockSpec((seq, LANES), lambda b: (b, 0)),
                   pl.BlockSpec((1, n_heads, seq), lambda b: (b, 0, 0))],
        out_shape=[jax.ShapeDtypeStruct((batch * seq, LANES), _F32),
                   jax.ShapeDtypeStruct((batch, n_heads, seq), _F32)],
        compiler_params=_params(("parallel",), _VMEM_MIB["fcumsum"]),
        name="fcumsum",
    )(flog, b_f_pad)


def _attn_kernel(q_ref, k_ref, v_ref, fcol_ref, frow_ref, w1_ref, w2_ref, w3_ref,
                 o_ref, w1b_ref, w2b_ref, w3b_ref, v2_scr, *, tq, tk, hps, ahead):
    w1b_ref[...] = w1_ref[...].astype(w1b_ref.dtype)
    w2b_ref[...] = w2_ref[...].astype(w2b_ref.dtype)
    w3b_ref[...] = w3_ref[...].astype(w3b_ref.dtype)

    hp = pl.program_id(1)
    S = q_ref.shape[1]
    n_blk = S // tq
    hsl = [slice(hh * HEAD_DIM, (hh + 1) * HEAD_DIM) for hh in range(hps)]
    den = slice(HEAD_DIM, 2 * HEAD_DIM)
    tiles = [(i, k0, min(k0 + tk, (i + 1) * tq))
             for i in reversed(range(n_blk)) for k0 in range(0, (i + 1) * tq, tk)]

    for hh in range(hps):
        v2_scr[hh, :, 0:HEAD_DIM] = v_ref[0, :, hsl[hh]]
        v2_scr[hh, :, den] = jnp.ones((S, HEAD_DIM), _BF16)

    lane = lax.broadcasted_iota(jnp.int32, (tq, LANES), 1)
    rows = lax.broadcasted_iota(jnp.int32, (tq, tq), 0)
    cols = lax.broadcasted_iota(jnp.int32, (tq, tq), 1)
    causal = cols <= rows

    half = tq // 2

    def parts(i, k0, k1):
        if k1 <= i * tq:
            return [(0, tq, k0, k1)]
        return [(0, half, k0, i * tq + half), (half, tq, k0, k1)]

    def scores(t):
        i, k0, k1 = tiles[t]
        for hh in range(hps):
            for r0, r1, c0, c1 in parts(i, k0, k1):
                s = lax.dot_general(q_ref[0, i * tq + r0:i * tq + r1, hsl[hh]],
                                    k_ref[0, c0:c1, hsl[hh]],
                                    (((1,), (1,)), ((), ())), preferred_element_type=_F32)
                pending[(t, hh, r0)] = s - frow_ref[0, 0, hh:hh + 1, c0:c1]

    pending = {}
    for t in range(min(ahead, len(tiles))):
        scores(t)
    fq = m = acc = None
    for t, (i, k0, k1) in enumerate(tiles):
        if t + ahead < len(tiles):
            scores(t + ahead)
        first, last = k0 == 0, k1 == (i + 1) * tq
        if first:
            fc = fcol_ref[0, i * tq:(i + 1) * tq, :]
            fq = [jnp.sum(jnp.where(lane == hp * hps + hh, fc, 0.0), axis=-1, keepdims=True)
                  for hh in range(hps)]
            m = [None] * hps
            acc = [None] * hps
        for hh in range(hps):
            new = []
            for r0, r1, c0, c1 in parts(i, k0, k1):
                a = pending.pop((t, hh, r0))
                if last:
                    d0 = i * tq - c0
                    diag = jnp.where(causal[r0:r1, 0:c1 - i * tq], a[:, d0:], -jnp.inf)
                    a = diag if d0 == 0 else jnp.concatenate([a[:, 0:d0], diag], axis=1)
                fq_r = fq[hh][r0:r1, :]
                m_tile = jnp.max(a, axis=-1, keepdims=True) + fq_r
                m_new = m_tile if first else jnp.maximum(m[hh][r0:r1, :], m_tile)
                p = jnp.exp2(a - (m_new - fq_r)).astype(_BF16)
                pv = jnp.dot(p, v2_scr[hh, c0:c1, :], preferred_element_type=_F32)
                acc_new = pv if first else (
                    jnp.exp2(m[hh][r0:r1, :] - m_new) * acc[hh][r0:r1, :] + pv)
                if last:
                    o_ref[0, i * tq + r0:i * tq + r1, hsl[hh]] = (
                        acc_new[:, 0:HEAD_DIM] / acc_new[:, den]).astype(o_ref.dtype)
                new.append((m_new, acc_new))
            if not last:
                m[hh], acc[hh] = new[0]


def _attention(q, k, v, fcol, frow, weights, *, tq=512, tk=2048, hps=2, ahead=1):
    B, S, W = q.shape
    H = W // HEAD_DIM
    n_hp = H // hps
    n_steps = B * n_hp
    heads = lambda b, h: (b, 0, h)
    slab = lambda b, h: (b * n_hp + h, 0)
    w_specs = []
    for w in weights:
        rows = w.shape[0] // n_steps
        assert rows * n_steps == w.shape[0]
        w_specs.append(pl.BlockSpec((rows, w.shape[1]), slab))
    return pl.pallas_call(
        functools.partial(_attn_kernel, tq=tq, tk=tk, hps=hps, ahead=ahead),
        grid=(B, n_hp),
        in_specs=[pl.BlockSpec((1, S, hps * HEAD_DIM), heads),
                  pl.BlockSpec((1, S, hps * HEAD_DIM), heads),
                  pl.BlockSpec((1, S, hps * HEAD_DIM), heads),
                  pl.BlockSpec((1, S, LANES), lambda b, h: (b, 0, 0)),
                  pl.BlockSpec((1, 1, hps, S), lambda b, h: (b, h, 0, 0))] + w_specs,
        out_specs=[pl.BlockSpec((1, S, hps * HEAD_DIM), heads)] + w_specs,
        out_shape=[jax.ShapeDtypeStruct((B, S, W), _BF16)]
                  + [jax.ShapeDtypeStruct(w.shape, _BF16) for w in weights],
        scratch_shapes=[pltpu.VMEM((hps, S, 2 * HEAD_DIM), _BF16)],
        compiler_params=_params(("parallel", "parallel"), _VMEM_MIB["attention"]),
        name="attention",
    )(q, k, v, fcol, frow.reshape(B, n_hp, hps, S), *weights)


def _outproj_kernel(a_ref, s_ref, w_ref, x_ref, mod_ref, o_ref):
    wa = a_ref.shape[1]
    mix = (jnp.dot(a_ref[...], w_ref[0:wa, :], preferred_element_type=_F32)
           + jnp.dot(s_ref[...], w_ref[wa:, :], preferred_element_type=_F32))
    o_ref[...] = x_ref[...] + mod_ref[0, _GATE1:_GATE1 + 1, :] * mix


def _outproj(attn, sgu, w_out, x2, mod3, *, seq, tm=512):
    M, D = x2.shape
    wa, ws = attn.shape[1], sgu.shape[1]
    tiles_per_seq = seq // tm
    row = lambda m: (m, 0)
    return pl.pallas_call(
        _outproj_kernel,
        grid=(M // tm,),
        in_specs=[pl.BlockSpec((tm, wa), row),
                  pl.BlockSpec((tm, ws), row),
                  pl.BlockSpec((wa + ws, D), lambda m: (0, 0)),
                  pl.BlockSpec((tm, D), row),
                  pl.BlockSpec((1, N_MOD, D), lambda m: (m // tiles_per_seq, 0, 0))],
        out_specs=pl.BlockSpec((tm, D), row),
        out_shape=jax.ShapeDtypeStruct((M, D), _F32),
        compiler_params=_params(("parallel",), _VMEM_MIB["outproj"]),
        name="outproj",
    )(attn, sgu, w_out, x2, mod3)


def _ffn_kernel(x_ref, mod_ref, g_ref, w1_ref, w2_ref, o_ref, h_scr):
    f = pl.program_id(1)

    @pl.when(f == 0)
    def _():
        _norm_modulate(x_ref, g_ref[...], mod_ref[0, _SHIFT2:_SHIFT2 + 1, :],
                       mod_ref[0, _SCALE2:_SCALE2 + 1, :], h_scr, copy_ref=o_ref)

    hid = jnp.dot(h_scr[...], w1_ref[...], preferred_element_type=_F32)
    hid = jnp.square(jnp.maximum(hid, 0.0)).astype(_BF16)
    part = jnp.dot(hid, w2_ref[...], preferred_element_type=_F32)
    o_ref[...] += mod_ref[0, _GATE2:_GATE2 + 1, :] * part


def _ffn(x2, mod3, g_ffn, w1, w2, *, seq, tm=512, tf=1024):
    M, D = x2.shape
    F = w1.shape[1]
    tiles_per_seq = seq // tm
    row = lambda m, f: (m, 0)
    return pl.pallas_call(
        _ffn_kernel,
        grid=(M // tm, F // tf),
        in_specs=[pl.BlockSpec((tm, D), row),
                  pl.BlockSpec((1, N_MOD, D), lambda m, f: (m // tiles_per_seq, 0, 0)),
                  pl.BlockSpec((1, D), lambda m, f: (0, 0)),
                  pl.BlockSpec((D, tf), lambda m, f: (0, f)),
                  pl.BlockSpec((tf, D), lambda m, f: (f, 0))],
        out_specs=pl.BlockSpec((tm, D), row),
        out_shape=jax.ShapeDtypeStruct((M, D), _F32),
        scratch_shapes=[pltpu.VMEM((tm, D), _BF16)],
        compiler_params=_params(("parallel", "arbitrary"), _VMEM_MIB["ffn"]),
        name="ffn",
    )(x2, mod3, g_ffn, w1, w2)


def kernel(x, c, w_ada, b_ada, g_mix, w_in, b_f, g_q, g_k, g_sgu, w_s, b_s, w_out, g_ffn,
           w_ff1, w_ff2):
    B, S, D = x.shape
    depth = w_ada.shape[0]
    H = b_f.shape[1]
    G = w_s.shape[1]
    attn_w = H * HEAD_DIM
    gmlp_w = G * HEAD_DIM
    x2 = x.reshape(B * S, D)
    for l in range(depth):
        o_f = 3 * attn_w
        o_u = o_f + H
        assert attn_w == gmlp_w and w_in.shape[2] == o_u + 2 * gmlp_w
        mod, w6 = _adaln(c, w_ada[l], b_ada[l], w_in[l].T, o_f=o_f, o_u=o_u)
        mod3 = mod.reshape(B, N_MOD, D)
        b_f_pad = jnp.pad(b_f[l], (0, LANES - H)).reshape(1, LANES)

        q, k, v, sgu, flog = _inproj(
            x2, mod3, g_mix[l].reshape(1, D), w6, g_q[l].reshape(1, HEAD_DIM),
            g_k[l].reshape(1, HEAD_DIM), g_sgu[l].reshape(1, gmlp_w), w_s[l], b_s[l], seq=S)

        fcol, frow = _fcumsum(flog, b_f_pad, batch=B, seq=S, n_heads=H)
        attn, w1b, w2b, wob = _attention(
            q.reshape(B, S, attn_w), k.reshape(B, S, attn_w), v.reshape(B, S, attn_w),
            fcol.reshape(B, S, LANES), frow, (w_ff1[l], w_ff2[l], w_out[l]))
        x2 = _outproj(attn.reshape(B * S, attn_w), sgu, wob, x2, mod3, seq=S)
        x2 = _ffn(x2, mod3, g_ffn[l].reshape(1, D), w1b, w2b, seq=S)
    return x2.reshape(B, S, D)
```

```python
import functools

import jax
import jax.numpy as jnp
import numpy as np
from jax import lax
from jax.experimental import pallas as pl
from jax.experimental.pallas import tpu as pltpu

_F32 = jnp.float32
_BF16 = jnp.bfloat16

HEAD_DIM = 128
CHUNK = 128
N_MOD = 6
EPS = 1e-6
LANES = 128
_LOG2E = np.float32(np.log2(np.e))

_SHIFT1, _SCALE1, _GATE1, _SHIFT2, _SCALE2, _GATE2 = range(6)

_MIB = 1024 * 1024
_VMEM_MIB = {"adaln": 40, "inproj": 56, "fcumsum": 32, "attention": 48, "outproj": 48, "ffn": 52}


def _params(semantics, vmem_mib):
    return pltpu.CompilerParams(dimension_semantics=semantics,
                                vmem_limit_bytes=vmem_mib * _MIB)


def _adaln_kernel(c_ref, w_ref, b_ref, win_ref, o_ref, w6_ref, *, o_f, o_u):
    c = c_ref[...]
    c_act = (c * jax.nn.sigmoid(c)).astype(_BF16)
    w = w_ref[...].astype(_BF16)
    o_ref[...] = jnp.dot(c_act, w, preferred_element_type=_F32) + b_ref[...]

    width = (win_ref.shape[0] - o_u) // 2
    w6_ref[0:width, :] = win_ref[o_u + width:, :].astype(w6_ref.dtype)
    w6_ref[width:2 * width, :] = win_ref[o_u:o_u + width, :].astype(w6_ref.dtype)
    w6_ref[2 * width:5 * width, :] = win_ref[0:o_f, :].astype(w6_ref.dtype)
    row = lax.broadcasted_iota(jnp.int32, (LANES, w6_ref.shape[1]), 0)
    w6_ref[5 * width:, :] = jnp.where(row < o_u - o_f, win_ref[o_f:o_f + LANES, :],
                                      0.0).astype(w6_ref.dtype)


def _adaln(c, w_ada, b_ada, w_in_t, *, o_f, o_u, n_steps=16):
    B, D = c.shape
    N = w_ada.shape[1]
    cols = w_in_t.shape[0]
    tn, td = N // n_steps, D // n_steps
    assert tn * n_steps == N and tn % LANES == 0 and td * n_steps == D and td % LANES == 0
    n_rows = cols - (o_u - o_f) + LANES
    dcol = lambda n: (0, n)
    return pl.pallas_call(
        functools.partial(_adaln_kernel, o_f=o_f, o_u=o_u),
        grid=(n_steps,),
        in_specs=[pl.BlockSpec((B, D), lambda n: (0, 0)),
                  pl.BlockSpec((D, tn), dcol),
                  pl.BlockSpec((1, tn), dcol),
                  pl.BlockSpec((cols, td), dcol)],
        out_specs=[pl.BlockSpec((B, tn), dcol),
                   pl.BlockSpec((n_rows, td), dcol)],
        out_shape=[jax.ShapeDtypeStruct((B, N), _F32),
                   jax.ShapeDtypeStruct((n_rows, D), _BF16)],
        compiler_params=_params(("parallel",), _VMEM_MIB["adaln"]),
        name="adaln",
    )(c, w_ada, b_ada.reshape(1, N), w_in_t)


_NORM_ROWS = 16


def _norm_modulate(x_ref, g, shift, scale, h_ref, copy_ref=None):
    one_plus_scale = 1.0 + scale
    for r in range(0, x_ref.shape[0], _NORM_ROWS):
        rows = slice(r, r + _NORM_ROWS)
        x = x_ref[rows, :]
        ms = jnp.mean(x * x, axis=-1, keepdims=True)
        y = x * lax.rsqrt(ms + EPS) * g
        h_ref[rows, :] = (y * one_plus_scale + shift).astype(h_ref.dtype)
        if copy_ref is not None:
            copy_ref[rows, :] = x


def _inproj_kernel(x_ref, mod_ref, gmix_ref, w_ref, gq_ref, gk_ref, gsgu_ref, ws_ref,
                   bs_ref, q_ref, k_ref, v_ref, sgu_ref, f_ref, h_scr, vn_scr, z_scr, *, q_scale):
    width = q_ref.shape[-1]
    n_heads = width // HEAD_DIM
    sec_vg, sec_u, sec_q, sec_k, sec_v = range(5)

    _norm_modulate(x_ref, gmix_ref[...], mod_ref[0, _SHIFT1:_SHIFT1 + 1, :],
                   mod_ref[0, _SCALE1:_SCALE1 + 1, :], h_scr)
    proj = lax.dot_general(h_scr[...], w_ref[...], (((1,), (1,)), ((), ())),
                           preferred_element_type=_F32)
    f_ref[...] = proj[:, 5 * width:]

    def section(idx):
        return proj[:, idx * width:(idx + 1) * width]

    def head_rms(acc, o_ref, g, scale):
        for hh in range(n_heads):
            sl = slice(hh * HEAD_DIM, (hh + 1) * HEAD_DIM)
            blk = acc[:, sl]
            ms = jnp.mean(blk * blk, axis=-1, keepdims=True)
            y = blk * lax.rsqrt(ms + EPS) * g
            if scale is not None:
                y = y * scale
            o_ref[:, sl] = y.astype(o_ref.dtype)

    acc = section(sec_vg)
    for gg in range(n_heads):
        sl = slice(gg * HEAD_DIM, (gg + 1) * HEAD_DIM)
        blk = jax.nn.gelu(acc[:, sl])
        mu = jnp.mean(blk, axis=-1, keepdims=True)
        xc = blk - mu
        var = jnp.mean(xc * xc, axis=-1, keepdims=True)
        y = xc * lax.rsqrt(var + EPS) * gsgu_ref[:, sl]
        vn_scr[:, sl] = y.astype(vn_scr.dtype)

    r = lax.broadcasted_iota(jnp.int32, (CHUNK, CHUNK), 0)
    cidx = lax.broadcasted_iota(jnp.int32, (CHUNK, CHUNK), 1)
    lower = cidx <= r
    for gg in range(n_heads):
        sl = slice(gg * HEAD_DIM, (gg + 1) * HEAD_DIM)
        w = jnp.where(lower, ws_ref[gg], 0.0).astype(_BF16)
        bcol = jnp.broadcast_to(bs_ref[gg:gg + 1, :], (CHUNK, CHUNK)).T
        for ci in range(x_ref.shape[0] // CHUNK):
            rs = slice(ci * CHUNK, (ci + 1) * CHUNK)
            z_scr[rs, sl] = jnp.dot(w, vn_scr[rs, sl], preferred_element_type=_F32) + bcol

    head_rms(section(sec_q), q_ref, gq_ref[...], q_scale)
    head_rms(section(sec_k), k_ref, gk_ref[...], None)
    sgu_ref[...] = (jax.nn.gelu(section(sec_u)) * z_scr[...]).astype(sgu_ref.dtype)
    v_ref[...] = section(sec_v).astype(v_ref.dtype)


def _inproj(x2, mod3, g_mix, w6, g_q, g_k, g_sgu, w_s, b_s, *, seq, tm=512):
    M, D = x2.shape
    n_sec = 5
    width = (w6.shape[0] - LANES) // n_sec
    tiles_per_seq = seq // tm
    row = lambda m: (m, 0)
    const2 = lambda m: (0, 0)
    resident = pl.Buffered(1)
    act = jax.ShapeDtypeStruct((M, width), _BF16)
    return pl.pallas_call(
        functools.partial(_inproj_kernel, q_scale=np.float32(_LOG2E / np.sqrt(HEAD_DIM))),
        grid=(M // tm,),
        in_specs=[pl.BlockSpec((tm, D), row),
                  pl.BlockSpec((1, N_MOD, D), lambda m: (m // tiles_per_seq, 0, 0)),
                  pl.BlockSpec((1, D), const2),
                  pl.BlockSpec(w6.shape, const2, pipeline_mode=resident),
                  pl.BlockSpec((1, HEAD_DIM), const2),
                  pl.BlockSpec((1, HEAD_DIM), const2),
                  pl.BlockSpec((1, width), const2),
                  pl.BlockSpec(w_s.shape, lambda m: (0, 0, 0)),
                  pl.BlockSpec(b_s.shape, const2)],
        out_specs=[pl.BlockSpec((tm, width), row)] * 4 + [pl.BlockSpec((tm, LANES), row)],
        out_shape=[act] * 4 + [jax.ShapeDtypeStruct((M, LANES), _F32)],
        scratch_shapes=[pltpu.VMEM((tm, D), _BF16), pltpu.VMEM((tm, width), _BF16),
                        pltpu.VMEM((tm, width), _F32)],
        compiler_params=_params(("parallel",), _VMEM_MIB["inproj"]),
        name="inproj",
    )(x2, mod3, g_mix, w6, g_q, g_k, g_sgu, w_s, b_s)


def _fcumsum_kernel(f_ref, bf_ref, fcol_ref, frow_ref, *, n_heads):
    S = f_ref.shape[0]
    r = lax.broadcasted_iota(jnp.int32, (CHUNK, CHUNK), 0)
    cidx = lax.broadcasted_iota(jnp.int32, (CHUNK, CHUNK), 1)
    tri = jnp.where(cidx <= r, 1.0, 0.0).astype(_BF16)
    carry = jnp.zeros((1, LANES), _F32)
    for ci in range(S // CHUNK):
        sl = slice(ci * CHUNK, (ci + 1) * CHUNK)
        lf = jax.nn.log_sigmoid(f_ref[sl, :] + bf_ref[...])
        hi = lf.astype(_BF16)
        r1 = lf - hi.astype(_F32)
        mid = r1.astype(_BF16)
        lo = (r1 - mid.astype(_F32)).astype(_BF16)
        cs = (jnp.dot(tri, hi, preferred_element_type=_F32)
              + jnp.dot(tri, mid, preferred_element_type=_F32)
              + jnp.dot(tri, lo, preferred_element_type=_F32))
        F = cs + carry
        F2 = F * _LOG2E
        fcol_ref[sl, :] = F2
        frow_ref[0, :, sl] = F2.T[0:n_heads, :]
        carry = F[CHUNK - 1:CHUNK, :]


def _fcumsum(flog, b_f_pad, *, batch, seq, n_heads):
    return pl.pallas_call(
        functools.partial(_fcumsum_kernel, n_heads=n_heads),
        grid=(batch,),
        in_specs=[pl.BlockSpec((seq, LANES), lambda b: (b, 0)),
                  pl.BlockSpec((1, LANES), lambda b: (0, 0))],
        out_specs=[pl.BlockSpec((seq, LANES), lambda b: (b, 0)),
                   pl.BlockSpec((1, n_heads, seq), lambda b: (b, 0, 0))],
        out_shape=[jax.ShapeDtypeStruct((batch * seq, LANES), _F32),
                   jax.ShapeDtypeStruct((batch, n_heads, seq), _F32)],
        compiler_params=_params(("parallel",), _VMEM_MIB["fcumsum"]),
        name="fcumsum",
    )(flog, b_f_pad)


def _attn_kernel(q_ref, k_ref, v_ref, fcol_ref, frow_ref, w1_ref, w2_ref, w3_ref,
                 o_ref, w1b_ref, w2b_ref, w3b_ref, v2_scr, *, tq, hps):
    w1b_ref[...] = w1_ref[...].astype(w1b_ref.dtype)
    w2b_ref[...] = w2_ref[...].astype(w2b_ref.dtype)
    w3b_ref[...] = w3_ref[...].astype(w3b_ref.dtype)

    hp = pl.program_id(1)
    S = q_ref.shape[1]
    n_blk = S // tq
    hsl = [slice(hh * HEAD_DIM, (hh + 1) * HEAD_DIM) for hh in range(hps)]
    den = slice(HEAD_DIM, 2 * HEAD_DIM)
    nt = (((1,), (1,)), ((), ()))

    for hh in range(hps):
        v2_scr[hh, :, 0:HEAD_DIM] = v_ref[0, :, hsl[hh]]
        v2_scr[hh, :, den] = jnp.ones((S, HEAD_DIM), _BF16)

    lane = lax.broadcasted_iota(jnp.int32, (tq, LANES), 1)
    rows = lax.broadcasted_iota(jnp.int32, (tq, tq), 0)
    cols = lax.broadcasted_iota(jnp.int32, (tq, tq), 1)
    causal = cols <= rows

    half = tq // 2

    def scores(i):
        d0 = i * tq
        for hh in range(hps):
            def qk(r0, r1, c0, c1):
                s = lax.dot_general(q_ref[0, d0 + r0:d0 + r1, hsl[hh]], k_ref[0, c0:c1, hsl[hh]],
                                    nt, preferred_element_type=_F32)
                return s - frow_ref[0, 0, hh:hh + 1, c0:c1]
            bulk = qk(0, tq, 0, d0) if i > 0 else None
            pending[(i, hh)] = (bulk, qk(0, half, d0, d0 + half), qk(half, tq, d0, d0 + tq))

    order = list(reversed(range(n_blk)))
    pending = {}
    scores(order[0])
    for n, i in enumerate(order):
        if n + 1 < n_blk:
            scores(order[n + 1])
        d0 = i * tq
        fc = fcol_ref[0, d0:d0 + tq, :]
        for hh in range(hps):
            fq = jnp.sum(jnp.where(lane == hp * hps + hh, fc, 0.0), axis=-1, keepdims=True)
            bulk, top, bot = pending.pop((i, hh))
            p = []
            for r0, diag, mask in ((0, top, causal[0:half, 0:half]), (half, bot, causal[half:tq, :])):
                diag = jnp.where(mask, diag, -jnp.inf)
                a = diag if bulk is None else jnp.concatenate([bulk[r0:r0 + half, :], diag], axis=1)
                fq_r = fq[r0:r0 + half, :]
                m_row = jnp.max(a, axis=-1, keepdims=True) + fq_r
                p.append(jnp.exp2(a - (m_row - fq_r)).astype(_BF16))
            acc = [jnp.dot(p[0][:, d0:], v2_scr[hh, d0:d0 + half, :], preferred_element_type=_F32),
                   jnp.dot(p[1][:, d0:], v2_scr[hh, d0:d0 + tq, :], preferred_element_type=_F32)]
            if bulk is not None:
                acc_bulk = jnp.dot(jnp.concatenate([p[0][:, 0:d0], p[1][:, 0:d0]], axis=0),
                                   v2_scr[hh, 0:d0, :], preferred_element_type=_F32)
                acc = [acc[0] + acc_bulk[0:half, :], acc[1] + acc_bulk[half:, :]]
            for r0, a2 in ((0, acc[0]), (half, acc[1])):
                o_ref[0, d0 + r0:d0 + r0 + half, hsl[hh]] = (
                    a2[:, 0:HEAD_DIM] / a2[:, den]).astype(o_ref.dtype)


def _attention(q, k, v, fcol, frow, weights, *, tq=512, hps=2):
    B, S, W = q.shape
    H = W // HEAD_DIM
    n_hp = H // hps
    n_steps = B * n_hp
    heads = lambda b, h: (b, 0, h)
    slab = lambda b, h: (b * n_hp + h, 0)
    w_specs = []
    for w in weights:
        rows = w.shape[0] // n_steps
        assert rows * n_steps == w.shape[0]
        w_specs.append(pl.BlockSpec((rows, w.shape[1]), slab))
    return pl.pallas_call(
        functools.partial(_attn_kernel, tq=tq, hps=hps),
        grid=(B, n_hp),
        in_specs=[pl.BlockSpec((1, S, hps * HEAD_DIM), heads),
                  pl.BlockSpec((1, S, hps * HEAD_DIM), heads),
                  pl.BlockSpec((1, S, hps * HEAD_DIM), heads),
                  pl.BlockSpec((1, S, LANES), lambda b, h: (b, 0, 0)),
                  pl.BlockSpec((1, 1, hps, S), lambda b, h: (b, h, 0, 0))] + w_specs,
        out_specs=[pl.BlockSpec((1, S, hps * HEAD_DIM), heads)] + w_specs,
        out_shape=[jax.ShapeDtypeStruct((B, S, W), _BF16)]
                  + [jax.ShapeDtypeStruct(w.shape, _BF16) for w in weights],
        scratch_shapes=[pltpu.VMEM((hps, S, 2 * HEAD_DIM), _BF16)],
        compiler_params=_params(("parallel", "parallel"), _VMEM_MIB["attention"]),
        name="attention",
    )(q, k, v, fcol, frow.reshape(B, n_hp, hps, S), *weights)


def _outproj_kernel(a_ref, s_ref, w_ref, x_ref, mod_ref, o_ref):
    wa = a_ref.shape[1]
    mix = (jnp.dot(a_ref[...], w_ref[0:wa, :], preferred_element_type=_F32)
           + jnp.dot(s_ref[...], w_ref[wa:, :], preferred_element_type=_F32))
    o_ref[...] = x_ref[...] + mod_ref[0, _GATE1:_GATE1 + 1, :] * mix


def _outproj(attn, sgu, w_out, x2, mod3, *, seq, tm=512):
    M, D = x2.shape
    wa, ws = attn.shape[1], sgu.shape[1]
    tiles_per_seq = seq // tm
    row = lambda m: (m, 0)
    return pl.pallas_call(
        _outproj_kernel,
        grid=(M // tm,),
        in_specs=[pl.BlockSpec((tm, wa), row),
                  pl.BlockSpec((tm, ws), row),
                  pl.BlockSpec((wa + ws, D), lambda m: (0, 0)),
                  pl.BlockSpec((tm, D), row),
                  pl.BlockSpec((1, N_MOD, D), lambda m: (m // tiles_per_seq, 0, 0))],
        out_specs=pl.BlockSpec((tm, D), row),
        out_shape=jax.ShapeDtypeStruct((M, D), _F32),
        compiler_params=_params(("parallel",), _VMEM_MIB["outproj"]),
        name="outproj",
    )(attn, sgu, w_out, x2, mod3)


def _ffn_kernel(x_ref, mod_ref, g_ref, w1_ref, w2_ref, o_ref, h_scr):
    f = pl.program_id(1)

    @pl.when(f == 0)
    def _():
        _norm_modulate(x_ref, g_ref[...], mod_ref[0, _SHIFT2:_SHIFT2 + 1, :],
                       mod_ref[0, _SCALE2:_SCALE2 + 1, :], h_scr, copy_ref=o_ref)

    hid = jnp.dot(h_scr[...], w1_ref[...], preferred_element_type=_F32)
    hid = jnp.square(jnp.maximum(hid, 0.0)).astype(_BF16)
    part = jnp.dot(hid, w2_ref[...], preferred_element_type=_F32)
    o_ref[...] += mod_ref[0, _GATE2:_GATE2 + 1, :] * part


def _ffn(x2, mod3, g_ffn, w1, w2, *, seq, tm=512, tf=1024):
    M, D = x2.shape
    F = w1.shape[1]
    tiles_per_seq = seq // tm
    row = lambda m, f: (m, 0)
    return pl.pallas_call(
        _ffn_kernel,
        grid=(M // tm, F // tf),
        in_specs=[pl.BlockSpec((tm, D), row),
                  pl.BlockSpec((1, N_MOD, D), lambda m, f: (m // tiles_per_seq, 0, 0)),
                  pl.BlockSpec((1, D), lambda m, f: (0, 0)),
                  pl.BlockSpec((D, tf), lambda m, f: (0, f)),
                  pl.BlockSpec((tf, D), lambda m, f: (f, 0))],
        out_specs=pl.BlockSpec((tm, D), row),
        out_shape=jax.ShapeDtypeStruct((M, D), _F32),
        scratch_shapes=[pltpu.VMEM((tm, D), _BF16)],
        compiler_params=_params(("parallel", "arbitrary"), _VMEM_MIB["ffn"]),
        name="ffn",
    )(x2, mod3, g_ffn, w1, w2)


def kernel(x, c, w_ada, b_ada, g_mix, w_in, b_f, g_q, g_k, g_sgu, w_s, b_s, w_out, g_ffn,
           w_ff1, w_ff2):
    B, S, D = x.shape
    depth = w_ada.shape[0]
    H = b_f.shape[1]
    G = w_s.shape[1]
    attn_w = H * HEAD_DIM
    gmlp_w = G * HEAD_DIM
    x2 = x.reshape(B * S, D)
    for l in range(depth):
        o_f = 3 * attn_w
        o_u = o_f + H
        assert attn_w == gmlp_w and w_in.shape[2] == o_u + 2 * gmlp_w
        mod, w6 = _adaln(c, w_ada[l], b_ada[l], w_in[l].T, o_f=o_f, o_u=o_u)
        mod3 = mod.reshape(B, N_MOD, D)
        b_f_pad = jnp.pad(b_f[l], (0, LANES - H)).reshape(1, LANES)

        q, k, v, sgu, flog = _inproj(
            x2, mod3, g_mix[l].reshape(1, D), w6, g_q[l].reshape(1, HEAD_DIM),
            g_k[l].reshape(1, HEAD_DIM), g_sgu[l].reshape(1, gmlp_w), w_s[l], b_s[l], seq=S)

        fcol, frow = _fcumsum(flog, b_f_pad, batch=B, seq=S, n_heads=H)
        attn, w1b, w2b, wob = _attention(
            q.reshape(B, S, attn_w), k.reshape(B, S, attn_w), v.reshape(B, S, attn_w),
            fcol.reshape(B, S, LANES), frow, (w_ff1[l], w_ff2[l], w_out[l]))
        x2 = _outproj(attn.reshape(B * S, attn_w), sgu, wob, x2, mod3, seq=S)
        x2 = _ffn(x2, mod3, g_ffn[l].reshape(1, D), w1b, w2b, seq=S)
    return x2.reshape(B, S, D)
```

```python
import functools

import jax
import jax.numpy as jnp
import numpy as np
from jax import lax
from jax.experimental import pallas as pl
from jax.experimental.pallas import tpu as pltpu

_F32 = jnp.float32
_BF16 = jnp.bfloat16

HEAD_DIM = 128
CHUNK = 128
N_MOD = 6
EPS = 1e-6
LANES = 128
_LOG2E = np.float32(np.log2(np.e))

_SHIFT1, _SCALE1, _GATE1, _SHIFT2, _SCALE2, _GATE2 = range(6)

_MIB = 1024 * 1024
_VMEM_MIB = {"adaln": 52, "inproj": 56, "fcumsum": 32, "attention": 48, "outproj": 48, "ffn": 52}


def _params(semantics, vmem_mib):
    return pltpu.CompilerParams(dimension_semantics=semantics,
                                vmem_limit_bytes=vmem_mib * _MIB)


def _adaln_kernel(c_ref, w_ref, b_ref, win_ref, o_ref, w6_ref, *, o_f, o_u):
    c = c_ref[...]
    c_act = (c * jax.nn.sigmoid(c)).astype(_BF16)
    w = w_ref[...].astype(_BF16)
    o_ref[...] = jnp.dot(c_act, w, preferred_element_type=_F32) + b_ref[...]

    width = (win_ref.shape[0] - o_u) // 2
    w6_ref[0:width, :] = win_ref[o_u + width:, :].astype(w6_ref.dtype)
    w6_ref[width:2 * width, :] = win_ref[o_u:o_u + width, :].astype(w6_ref.dtype)
    w6_ref[2 * width:5 * width, :] = win_ref[0:o_f, :].astype(w6_ref.dtype)
    row = lax.broadcasted_iota(jnp.int32, (LANES, w6_ref.shape[1]), 0)
    w6_ref[5 * width:, :] = jnp.where(row < o_u - o_f, win_ref[o_f:o_f + LANES, :],
                                      0.0).astype(w6_ref.dtype)


def _adaln(c, w_ada, b_ada, w_in_t, *, o_f, o_u, n_steps=8):
    B, D = c.shape
    N = w_ada.shape[1]
    cols = w_in_t.shape[0]
    tn, td = N // n_steps, D // n_steps
    assert tn * n_steps == N and tn % LANES == 0 and td * n_steps == D and td % LANES == 0
    n_rows = cols - (o_u - o_f) + LANES
    dcol = lambda n: (0, n)
    return pl.pallas_call(
        functools.partial(_adaln_kernel, o_f=o_f, o_u=o_u),
        grid=(n_steps,),
        in_specs=[pl.BlockSpec((B, D), lambda n: (0, 0)),
                  pl.BlockSpec((D, tn), dcol),
                  pl.BlockSpec((1, tn), dcol),
                  pl.BlockSpec((cols, td), dcol)],
        out_specs=[pl.BlockSpec((B, tn), dcol),
                   pl.BlockSpec((n_rows, td), dcol)],
        out_shape=[jax.ShapeDtypeStruct((B, N), _F32),
                   jax.ShapeDtypeStruct((n_rows, D), _BF16)],
        compiler_params=_params(("parallel",), _VMEM_MIB["adaln"]),
        name="adaln",
    )(c, w_ada, b_ada.reshape(1, N), w_in_t)


_NORM_ROWS = 16


def _norm_modulate(x_ref, g, shift, scale, h_ref, copy_ref=None):
    gain = g * (1.0 + scale)
    for r in range(0, x_ref.shape[0], _NORM_ROWS):
        rows = slice(r, r + _NORM_ROWS)
        x = x_ref[rows, :]
        ms = jnp.mean(x * x, axis=-1, keepdims=True)
        h_ref[rows, :] = (x * lax.rsqrt(ms + EPS) * gain + shift).astype(h_ref.dtype)
        if copy_ref is not None:
            copy_ref[rows, :] = x


def _inproj_kernel(x_ref, mod_ref, gmix_ref, w_ref, gq_ref, gk_ref, gsgu_ref, ws_ref,
                   bs_ref, q_ref, k_ref, v_ref, sgu_ref, f_ref, h_scr, vn_scr, z_scr, *, q_scale):
    width = q_ref.shape[-1]
    n_heads = width // HEAD_DIM
    sec_vg, sec_u, sec_q, sec_k, sec_v = range(5)

    _norm_modulate(x_ref, gmix_ref[...], mod_ref[0, _SHIFT1:_SHIFT1 + 1, :],
                   mod_ref[0, _SCALE1:_SCALE1 + 1, :], h_scr)
    proj = lax.dot_general(h_scr[...], w_ref[...], (((1,), (1,)), ((), ())),
                           preferred_element_type=_F32)
    f_ref[...] = proj[:, 5 * width:]

    def section(idx):
        return proj[:, idx * width:(idx + 1) * width]

    def head_rms(acc, o_ref, g, scale):
        for hh in range(n_heads):
            sl = slice(hh * HEAD_DIM, (hh + 1) * HEAD_DIM)
            blk = acc[:, sl]
            ms = jnp.mean(blk * blk, axis=-1, keepdims=True)
            y = blk * lax.rsqrt(ms + EPS) * g
            if scale is not None:
                y = y * scale
            o_ref[:, sl] = y.astype(o_ref.dtype)

    acc = section(sec_vg)
    for gg in range(n_heads):
        sl = slice(gg * HEAD_DIM, (gg + 1) * HEAD_DIM)
        blk = jax.nn.gelu(acc[:, sl])
        mu = jnp.mean(blk, axis=-1, keepdims=True)
        xc = blk - mu
        var = jnp.mean(xc * xc, axis=-1, keepdims=True)
        y = xc * lax.rsqrt(var + EPS) * gsgu_ref[:, sl]
        vn_scr[:, sl] = y.astype(vn_scr.dtype)

    r = lax.broadcasted_iota(jnp.int32, (CHUNK, CHUNK), 0)
    cidx = lax.broadcasted_iota(jnp.int32, (CHUNK, CHUNK), 1)
    lower = cidx <= r
    for gg in range(n_heads):
        sl = slice(gg * HEAD_DIM, (gg + 1) * HEAD_DIM)
        w = jnp.where(lower, ws_ref[gg], 0.0).astype(_BF16)
        bcol = jnp.broadcast_to(bs_ref[gg:gg + 1, :], (CHUNK, CHUNK)).T
        for ci in range(x_ref.shape[0] // CHUNK):
            rs = slice(ci * CHUNK, (ci + 1) * CHUNK)
            z_scr[rs, sl] = jnp.dot(w, vn_scr[rs, sl], preferred_element_type=_F32) + bcol

    head_rms(section(sec_q), q_ref, gq_ref[...], q_scale)
    head_rms(section(sec_k), k_ref, gk_ref[...], None)
    sgu_ref[...] = (jax.nn.gelu(section(sec_u)) * z_scr[...]).astype(sgu_ref.dtype)
    v_ref[...] = section(sec_v).astype(v_ref.dtype)


def _inproj(x2, mod3, g_mix, w6, g_q, g_k, g_sgu, w_s, b_s, *, seq, tm=512):
    M, D = x2.shape
    n_sec = 5
    width = (w6.shape[0] - LANES) // n_sec
    tiles_per_seq = seq // tm
    row = lambda m: (m, 0)
    const2 = lambda m: (0, 0)
    resident = pl.Buffered(1)
    act = jax.ShapeDtypeStruct((M, width), _BF16)
    return pl.pallas_call(
        functools.partial(_inproj_kernel, q_scale=np.float32(_LOG2E / np.sqrt(HEAD_DIM))),
        grid=(M // tm,),
        in_specs=[pl.BlockSpec((tm, D), row),
                  pl.BlockSpec((1, N_MOD, D), lambda m: (m // tiles_per_seq, 0, 0)),
                  pl.BlockSpec((1, D), const2),
                  pl.BlockSpec(w6.shape, const2, pipeline_mode=resident),
                  pl.BlockSpec((1, HEAD_DIM), const2),
                  pl.BlockSpec((1, HEAD_DIM), const2),
                  pl.BlockSpec((1, width), const2),
                  pl.BlockSpec(w_s.shape, lambda m: (0, 0, 0)),
                  pl.BlockSpec(b_s.shape, const2)],
        out_specs=[pl.BlockSpec((tm, width), row)] * 4 + [pl.BlockSpec((tm, LANES), row)],
        out_shape=[act] * 4 + [jax.ShapeDtypeStruct((M, LANES), _F32)],
        scratch_shapes=[pltpu.VMEM((tm, D), _BF16), pltpu.VMEM((tm, width), _BF16),
                        pltpu.VMEM((tm, width), _F32)],
        compiler_params=_params(("parallel",), _VMEM_MIB["inproj"]),
        name="inproj",
    )(x2, mod3, g_mix, w6, g_q, g_k, g_sgu, w_s, b_s)


def _fcumsum_kernel(f_ref, bf_ref, fcol_ref, frow_ref, *, n_heads):
    S = f_ref.shape[0]
    r = lax.broadcasted_iota(jnp.int32, (CHUNK, CHUNK), 0)
    cidx = lax.broadcasted_iota(jnp.int32, (CHUNK, CHUNK), 1)
    tri = jnp.where(cidx <= r, 1.0, 0.0).astype(_BF16)
    carry = jnp.zeros((1, LANES), _F32)
    for ci in range(S // CHUNK):
        sl = slice(ci * CHUNK, (ci + 1) * CHUNK)
        lf = jax.nn.log_sigmoid(f_ref[sl, :] + bf_ref[...])
        hi = lf.astype(_BF16)
        r1 = lf - hi.astype(_F32)
        mid = r1.astype(_BF16)
        lo = (r1 - mid.astype(_F32)).astype(_BF16)
        cs = (jnp.dot(tri, hi, preferred_element_type=_F32)
              + jnp.dot(tri, mid, preferred_element_type=_F32)
              + jnp.dot(tri, lo, preferred_element_type=_F32))
        F = cs + carry
        F2 = F * _LOG2E
        fcol_ref[sl, :] = F2
        frow_ref[0, :, sl] = F2.T[0:n_heads, :]
        carry = F[CHUNK - 1:CHUNK, :]


def _fcumsum(flog, b_f_pad, *, batch, seq, n_heads):
    return pl.pallas_call(
        functools.partial(_fcumsum_kernel, n_heads=n_heads),
        grid=(batch,),
        in_specs=[pl.BlockSpec((seq, LANES), lambda b: (b, 0)),
                  pl.BlockSpec((1, LANES), lambda b: (0, 0))],
        out_specs=[pl.BlockSpec((seq, LANES), lambda b: (b, 0)),
                   pl.BlockSpec((1, n_heads, seq), lambda b: (b, 0, 0))],
        out_shape=[jax.ShapeDtypeStruct((batch * seq, LANES), _F32),
                   jax.ShapeDtypeStruct((batch, n_heads, seq), _F32)],
        compiler_params=_params(("parallel",), _VMEM_MIB["fcumsum"]),
        name="fcumsum",
    )(flog, b_f_pad)


def _attn_kernel(q_ref, k_ref, v_ref, fcol_ref, frow_ref, w1_ref, w2_ref, w3_ref,
                 o_ref, w1b_ref, w2b_ref, w3b_ref, v2_scr, *, tq, tk, hps, ahead):
    w1b_ref[...] = w1_ref[...].astype(w1b_ref.dtype)
    w2b_ref[...] = w2_ref[...].astype(w2b_ref.dtype)
    w3b_ref[...] = w3_ref[...].astype(w3b_ref.dtype)

    hp = pl.program_id(1)
    S = q_ref.shape[1]
    n_blk = S // tq
    hsl = [slice(hh * HEAD_DIM, (hh + 1) * HEAD_DIM) for hh in range(hps)]
    den = slice(HEAD_DIM, 2 * HEAD_DIM)
    tiles = [(i, k0, min(k0 + tk, (i + 1) * tq))
             for i in reversed(range(n_blk)) for k0 in range(0, (i + 1) * tq, tk)]

    for hh in range(hps):
        v2_scr[hh, :, 0:HEAD_DIM] = v_ref[0, :, hsl[hh]]
        v2_scr[hh, :, den] = jnp.ones((S, HEAD_DIM), _BF16)

    lane = lax.broadcasted_iota(jnp.int32, (tq, LANES), 1)
    rows = lax.broadcasted_iota(jnp.int32, (tq, tq), 0)
    cols = lax.broadcasted_iota(jnp.int32, (tq, tq), 1)
    causal = cols <= rows

    half = tq // 2

    def parts(i, k0, k1):
        if k1 <= i * tq:
            return [(0, tq, k0, k1)]
        return [(0, half, k0, i * tq + half), (half, tq, k0, k1)]

    def scores(t):
        i, k0, k1 = tiles[t]
        for hh in range(hps):
            for r0, r1, c0, c1 in parts(i, k0, k1):
                s = lax.dot_general(q_ref[0, i * tq + r0:i * tq + r1, hsl[hh]],
                                    k_ref[0, c0:c1, hsl[hh]],
                                    (((1,), (1,)), ((), ())), preferred_element_type=_F32)
                pending[(t, hh, r0)] = s - frow_ref[0, 0, hh:hh + 1, c0:c1]

    pending = {}
    for t in range(min(ahead, len(tiles))):
        scores(t)
    fq = m = acc = None
    for t, (i, k0, k1) in enumerate(tiles):
        if t + ahead < len(tiles):
            scores(t + ahead)
        first, last = k0 == 0, k1 == (i + 1) * tq
        if first:
            fc = fcol_ref[0, i * tq:(i + 1) * tq, :]
            fq = [jnp.sum(jnp.where(lane == hp * hps + hh, fc, 0.0), axis=-1, keepdims=True)
                  for hh in range(hps)]
            m = [None] * hps
            acc = [None] * hps
        for hh in range(hps):
            new = []
            for r0, r1, c0, c1 in parts(i, k0, k1):
                a = pending.pop((t, hh, r0))
                if last:
                    d0 = i * tq - c0
                    diag = jnp.where(causal[r0:r1, 0:c1 - i * tq], a[:, d0:], -jnp.inf)
                    a = diag if d0 == 0 else jnp.concatenate([a[:, 0:d0], diag], axis=1)
                fq_r = fq[hh][r0:r1, :]
                m_tile = jnp.max(a, axis=-1, keepdims=True) + fq_r
                m_new = m_tile if first else jnp.maximum(m[hh][r0:r1, :], m_tile)
                p = jnp.exp2(a - (m_new - fq_r)).astype(_BF16)
                pv = jnp.dot(p, v2_scr[hh, c0:c1, :], preferred_element_type=_F32)
                acc_new = pv if first else (
                    jnp.exp2(m[hh][r0:r1, :] - m_new) * acc[hh][r0:r1, :] + pv)
                if last:
                    o_ref[0, i * tq + r0:i * tq + r1, hsl[hh]] = (
                        acc_new[:, 0:HEAD_DIM] / acc_new[:, den]).astype(o_ref.dtype)
                new.append((m_new, acc_new))
            if not last:
                m[hh], acc[hh] = new[0]


def _attention(q, k, v, fcol, frow, weights, *, tq=512, tk=2048, hps=2, ahead=2):
    B, S, W = q.shape
    H = W // HEAD_DIM
    n_hp = H // hps
    n_steps = B * n_hp
    heads = lambda b, h: (b, 0, h)
    slab = lambda b, h: (b * n_hp + h, 0)
    w_specs = []
    for w in weights:
        rows = w.shape[0] // n_steps
        assert rows * n_steps == w.shape[0]
        w_specs.append(pl.BlockSpec((rows, w.shape[1]), slab))
    return pl.pallas_call(
        functools.partial(_attn_kernel, tq=tq, tk=tk, hps=hps, ahead=ahead),
        grid=(B, n_hp),
        in_specs=[pl.BlockSpec((1, S, hps * HEAD_DIM), heads),
                  pl.BlockSpec((1, S, hps * HEAD_DIM), heads),
                  pl.BlockSpec((1, S, hps * HEAD_DIM), heads),
                  pl.BlockSpec((1, S, LANES), lambda b, h: (b, 0, 0)),
                  pl.BlockSpec((1, 1, hps, S), lambda b, h: (b, h, 0, 0))] + w_specs,
        out_specs=[pl.BlockSpec((1, S, hps * HEAD_DIM), heads)] + w_specs,
        out_shape=[jax.ShapeDtypeStruct((B, S, W), _BF16)]
                  + [jax.ShapeDtypeStruct(w.shape, _BF16) for w in weights],
        scratch_shapes=[pltpu.VMEM((hps, S, 2 * HEAD_DIM), _BF16)],
        compiler_params=_params(("parallel", "parallel"), _VMEM_MIB["attention"]),
        name="attention",
    )(q, k, v, fcol, frow.reshape(B, n_hp, hps, S), *weights)


def _outproj_kernel(a_ref, s_ref, w_ref, x_ref, mod_ref, o_ref):
    wa = a_ref.shape[1]
    mix = (jnp.dot(a_ref[...], w_ref[0:wa, :], preferred_element_type=_F32)
           + jnp.dot(s_ref[...], w_ref[wa:, :], preferred_element_type=_F32))
    o_ref[...] = x_ref[...] + mod_ref[0, _GATE1:_GATE1 + 1, :] * mix


def _outproj(attn, sgu, w_out, x2, mod3, *, seq, tm=512):
    M, D = x2.shape
    wa, ws = attn.shape[1], sgu.shape[1]
    tiles_per_seq = seq // tm
    row = lambda m: (m, 0)
    return pl.pallas_call(
        _outproj_kernel,
        grid=(M // tm,),
        in_specs=[pl.BlockSpec((tm, wa), row),
                  pl.BlockSpec((tm, ws), row),
                  pl.BlockSpec((wa + ws, D), lambda m: (0, 0)),
                  pl.BlockSpec((tm, D), row),
                  pl.BlockSpec((1, N_MOD, D), lambda m: (m // tiles_per_seq, 0, 0))],
        out_specs=pl.BlockSpec((tm, D), row),
        out_shape=jax.ShapeDtypeStruct((M, D), _F32),
        compiler_params=_params(("parallel",), _VMEM_MIB["outproj"]),
        name="outproj",
    )(attn, sgu, w_out, x2, mod3)


def _ffn_kernel(x_ref, mod_ref, g_ref, w1_ref, w2_ref, o_ref, h_scr):
    f = pl.program_id(1)

    @pl.when(f == 0)
    def _():
        _norm_modulate(x_ref, g_ref[...], mod_ref[0, _SHIFT2:_SHIFT2 + 1, :],
                       mod_ref[0, _SCALE2:_SCALE2 + 1, :], h_scr, copy_ref=o_ref)

    hid = jnp.dot(h_scr[...], w1_ref[...], preferred_element_type=_F32)
    hid = jnp.square(jnp.maximum(hid, 0.0)).astype(_BF16)
    part = jnp.dot(hid, w2_ref[...], preferred_element_type=_F32)
    o_ref[...] += mod_ref[0, _GATE2:_GATE2 + 1, :] * part


def _ffn(x2, mod3, g_ffn, w1, w2, *, seq, tm=512, tf=1024):
    M, D = x2.shape
    F = w1.shape[1]
    tiles_per_seq = seq // tm
    row = lambda m, f: (m, 0)
    return pl.pallas_call(
        _ffn_kernel,
        grid=(M // tm, F // tf),
        in_specs=[pl.BlockSpec((tm, D), row),
                  pl.BlockSpec((1, N_MOD, D), lambda m, f: (m // tiles_per_seq, 0, 0)),
                  pl.BlockSpec((1, D), lambda m, f: (0, 0)),
                  pl.BlockSpec((D, tf), lambda m, f: (0, f)),
                  pl.BlockSpec((tf, D), lambda m, f: (f, 0))],
        out_specs=pl.BlockSpec((tm, D), row),
        out_shape=jax.ShapeDtypeStruct((M, D), _F32),
        scratch_shapes=[pltpu.VMEM((tm, D), _BF16)],
        compiler_params=_params(("parallel", "arbitrary"), _VMEM_MIB["ffn"]),
        name="ffn",
    )(x2, mod3, g_ffn, w1, w2)


def kernel(x, c, w_ada, b_ada, g_mix, w_in, b_f, g_q, g_k, g_sgu, w_s, b_s, w_out, g_ffn,
           w_ff1, w_ff2):
    B, S, D = x.shape
    depth = w_ada.shape[0]
    H = b_f.shape[1]
    G = w_s.shape[1]
    attn_w = H * HEAD_DIM
    gmlp_w = G * HEAD_DIM
    x2 = x.reshape(B * S, D)
    for l in range(depth):
        o_f = 3 * attn_w
        o_u = o_f + H
        assert attn_w == gmlp_w and w_in.shape[2] == o_u + 2 * gmlp_w
        mod, w6 = _adaln(c, w_ada[l], b_ada[l], w_in[l].T, o_f=o_f, o_u=o_u)
        mod3 = mod.reshape(B, N_MOD, D)
        b_f_pad = jnp.pad(b_f[l], (0, LANES - H)).reshape(1, LANES)

        q, k, v, sgu, flog = _inproj(
            x2, mod3, g_mix[l].reshape(1, D), w6, g_q[l].reshape(1, HEAD_DIM),
            g_k[l].reshape(1, HEAD_DIM), g_sgu[l].reshape(1, gmlp_w), w_s[l], b_s[l], seq=S)

        fcol, frow = _fcumsum(flog, b_f_pad, batch=B, seq=S, n_heads=H)
        attn, w1b, w2b, wob = _attention(
            q.reshape(B, S, attn_w), k.reshape(B, S, attn_w), v.reshape(B, S, attn_w),
            fcol.reshape(B, S, LANES), frow, (w_ff1[l], w_ff2[l], w_out[l]))
        x2 = _outproj(attn.reshape(B * S, attn_w), sgu, wob, x2, mod3, seq=S)
        x2 = _ffn(x2, mod3, g_ffn[l].reshape(1, D), w1b, w2b, seq=S)
    return x2.reshape(B, S, D)
```

```python
import functools

import jax
import jax.numpy as jnp
import numpy as np
from jax import lax
from jax.experimental import pallas as pl
from jax.experimental.pallas import tpu as pltpu

_F32 = jnp.float32
_BF16 = jnp.bfloat16

HEAD_DIM = 128
CHUNK = 128
N_MOD = 6
EPS = 1e-6
LANES = 128
_LOG2E = np.float32(np.log2(np.e))

_SHIFT1, _SCALE1, _GATE1, _SHIFT2, _SCALE2, _GATE2 = range(6)

_MIB = 1024 * 1024
_VMEM_MIB = {"adaln": 40, "inproj": 56, "fcumsum": 32, "attention": 48, "outproj": 48, "ffn": 52}


def _params(semantics, vmem_mib):
    return pltpu.CompilerParams(dimension_semantics=semantics,
                                vmem_limit_bytes=vmem_mib * _MIB)


def _adaln_kernel(c_ref, w_ref, b_ref, win_ref, o_ref, w6_ref, *, o_f, o_u):
    c = c_ref[...]
    c_act = (c * jax.nn.sigmoid(c)).astype(_BF16)
    w = w_ref[...].astype(_BF16)
    o_ref[...] = jnp.dot(c_act, w, preferred_element_type=_F32) + b_ref[...]

    width = (win_ref.shape[0] - o_u) // 2
    w6_ref[0:width, :] = win_ref[o_u + width:, :].astype(w6_ref.dtype)
    w6_ref[width:2 * width, :] = win_ref[o_u:o_u + width, :].astype(w6_ref.dtype)
    w6_ref[2 * width:5 * width, :] = win_ref[0:o_f, :].astype(w6_ref.dtype)
    row = lax.broadcasted_iota(jnp.int32, (LANES, w6_ref.shape[1]), 0)
    w6_ref[5 * width:, :] = jnp.where(row < o_u - o_f, win_ref[o_f:o_f + LANES, :],
                                      0.0).astype(w6_ref.dtype)


def _adaln(c, w_ada, b_ada, w_in_t, *, o_f, o_u, n_steps=16):
    B, D = c.shape
    N = w_ada.shape[1]
    cols = w_in_t.shape[0]
    tn, td = N // n_steps, D // n_steps
    assert tn * n_steps == N and tn % LANES == 0 and td * n_steps == D and td % LANES == 0
    n_rows = cols - (o_u - o_f) + LANES
    dcol = lambda n: (0, n)
    return pl.pallas_call(
        functools.partial(_adaln_kernel, o_f=o_f, o_u=o_u),
        grid=(n_steps,),
        in_specs=[pl.BlockSpec((B, D), lambda n: (0, 0)),
                  pl.BlockSpec((D, tn), dcol),
                  pl.BlockSpec((1, tn), dcol),
                  pl.BlockSpec((cols, td), dcol)],
        out_specs=[pl.BlockSpec((B, tn), dcol),
                   pl.BlockSpec((n_rows, td), dcol)],
        out_shape=[jax.ShapeDtypeStruct((B, N), _F32),
                   jax.ShapeDtypeStruct((n_rows, D), _BF16)],
        compiler_params=_params(("parallel",), _VMEM_MIB["adaln"]),
        name="adaln",
    )(c, w_ada, b_ada.reshape(1, N), w_in_t)


_NORM_ROWS = 16


def _norm_modulate(x_ref, g, shift, scale, h_ref, copy_ref=None):
    gain = g * (1.0 + scale)
    for r in range(0, x_ref.shape[0], _NORM_ROWS):
        rows = slice(r, r + _NORM_ROWS)
        x = x_ref[rows, :]
        ms = jnp.mean(x * x, axis=-1, keepdims=True)
        h_ref[rows, :] = (x * lax.rsqrt(ms + EPS) * gain + shift).astype(h_ref.dtype)
        if copy_ref is not None:
            copy_ref[rows, :] = x


def _inproj_kernel(x_ref, mod_ref, gmix_ref, w_ref, gq_ref, gk_ref, gsgu_ref, ws_ref,
                   bs_ref, qkv_ref, sgu_ref, f_ref, h_scr, vn_scr, z_scr, *, q_scale):
    width = sgu_ref.shape[-1]
    n_heads = width // HEAD_DIM
    q_ref = qkv_ref.at[:, 0:width]
    k_ref = qkv_ref.at[:, width:2 * width]
    v_ref = qkv_ref.at[:, 2 * width:3 * width]
    sec_vg, sec_u, sec_q, sec_k, sec_v = range(5)

    _norm_modulate(x_ref, gmix_ref[...], mod_ref[0, _SHIFT1:_SHIFT1 + 1, :],
                   mod_ref[0, _SCALE1:_SCALE1 + 1, :], h_scr)
    proj = lax.dot_general(h_scr[...], w_ref[...], (((1,), (1,)), ((), ())),
                           preferred_element_type=_F32)
    f_ref[...] = proj[:, 5 * width:]

    def section(idx):
        return proj[:, idx * width:(idx + 1) * width]

    def head_rms(acc, o_ref, g, scale):
        for hh in range(n_heads):
            sl = slice(hh * HEAD_DIM, (hh + 1) * HEAD_DIM)
            blk = acc[:, sl]
            ms = jnp.mean(blk * blk, axis=-1, keepdims=True)
            y = blk * lax.rsqrt(ms + EPS) * g
            if scale is not None:
                y = y * scale
            o_ref[:, sl] = y.astype(o_ref.dtype)

    acc = section(sec_vg)
    for gg in range(n_heads):
        sl = slice(gg * HEAD_DIM, (gg + 1) * HEAD_DIM)
        blk = jax.nn.gelu(acc[:, sl])
        mu = jnp.mean(blk, axis=-1, keepdims=True)
        xc = blk - mu
        var = jnp.mean(xc * xc, axis=-1, keepdims=True)
        y = xc * lax.rsqrt(var + EPS) * gsgu_ref[:, sl]
        vn_scr[:, sl] = y.astype(vn_scr.dtype)

    r = lax.broadcasted_iota(jnp.int32, (CHUNK, CHUNK), 0)
    cidx = lax.broadcasted_iota(jnp.int32, (CHUNK, CHUNK), 1)
    lower = cidx <= r
    for gg in range(n_heads):
        sl = slice(gg * HEAD_DIM, (gg + 1) * HEAD_DIM)
        w = jnp.where(lower, ws_ref[gg], 0.0).astype(_BF16)
        bcol = jnp.broadcast_to(bs_ref[gg:gg + 1, :], (CHUNK, CHUNK)).T
        for ci in range(x_ref.shape[0] // CHUNK):
            rs = slice(ci * CHUNK, (ci + 1) * CHUNK)
            z_scr[rs, sl] = jnp.dot(w, vn_scr[rs, sl], preferred_element_type=_F32) + bcol

    head_rms(section(sec_q), q_ref, gq_ref[...], q_scale)
    head_rms(section(sec_k), k_ref, gk_ref[...], None)
    sgu_ref[...] = (jax.nn.gelu(section(sec_u)) * z_scr[...]).astype(sgu_ref.dtype)
    v_ref[...] = section(sec_v).astype(v_ref.dtype)


def _inproj(x2, mod3, g_mix, w6, g_q, g_k, g_sgu, w_s, b_s, *, seq, tm=512):
    M, D = x2.shape
    n_sec = 5
    width = (w6.shape[0] - LANES) // n_sec
    tiles_per_seq = seq // tm
    row = lambda m: (m, 0)
    const2 = lambda m: (0, 0)
    resident = pl.Buffered(1)
    act = jax.ShapeDtypeStruct((M, width), _BF16)
    return pl.pallas_call(
        functools.partial(_inproj_kernel, q_scale=np.float32(_LOG2E / np.sqrt(HEAD_DIM))),
        grid=(M // tm,),
        in_specs=[pl.BlockSpec((tm, D), row),
                  pl.BlockSpec((1, N_MOD, D), lambda m: (m // tiles_per_seq, 0, 0)),
                  pl.BlockSpec((1, D), const2),
                  pl.BlockSpec(w6.shape, const2, pipeline_mode=resident),
                  pl.BlockSpec((1, HEAD_DIM), const2),
                  pl.BlockSpec((1, HEAD_DIM), const2),
                  pl.BlockSpec((1, width), const2),
                  pl.BlockSpec(w_s.shape, lambda m: (0, 0, 0)),
                  pl.BlockSpec(b_s.shape, const2)],
        out_specs=[pl.BlockSpec((tm, 3 * width), row), pl.BlockSpec((tm, width), row),
                   pl.BlockSpec((tm, LANES), row)],
        out_shape=[jax.ShapeDtypeStruct((M, 3 * width), _BF16), act,
                   jax.ShapeDtypeStruct((M, LANES), _F32)],
        scratch_shapes=[pltpu.VMEM((tm, D), _BF16), pltpu.VMEM((tm, width), _BF16),
                        pltpu.VMEM((tm, width), _F32)],
        compiler_params=_params(("parallel",), _VMEM_MIB["inproj"]),
        name="inproj",
    )(x2, mod3, g_mix, w6, g_q, g_k, g_sgu, w_s, b_s)


def _fcumsum_kernel(f_ref, bf_ref, frow_ref, *, n_heads):
    S = f_ref.shape[0]
    r = lax.broadcasted_iota(jnp.int32, (CHUNK, CHUNK), 0)
    cidx = lax.broadcasted_iota(jnp.int32, (CHUNK, CHUNK), 1)
    tri = jnp.where(cidx <= r, 1.0, 0.0).astype(_BF16)
    carry = jnp.zeros((1, LANES), _F32)
    for ci in range(S // CHUNK):
        sl = slice(ci * CHUNK, (ci + 1) * CHUNK)
        lf = jax.nn.log_sigmoid(f_ref[sl, :] + bf_ref[...])
        hi = lf.astype(_BF16)
        r1 = lf - hi.astype(_F32)
        mid = r1.astype(_BF16)
        lo = (r1 - mid.astype(_F32)).astype(_BF16)
        cs = (jnp.dot(tri, hi, preferred_element_type=_F32)
              + jnp.dot(tri, mid, preferred_element_type=_F32)
              + jnp.dot(tri, lo, preferred_element_type=_F32))
        F = cs + carry
        frow_ref[0, :, sl] = (F * _LOG2E).T[0:n_heads, :]
        carry = F[CHUNK - 1:CHUNK, :]


def _fcumsum(flog, b_f_pad, *, batch, seq, n_heads):
    return pl.pallas_call(
        functools.partial(_fcumsum_kernel, n_heads=n_heads),
        grid=(batch,),
        in_specs=[pl.BlockSpec((seq, LANES), lambda b: (b, 0)),
                  pl.BlockSpec((1, LANES), lambda b: (0, 0))],
        out_specs=pl.BlockSpec((1, n_heads, seq), lambda b: (b, 0, 0)),
        out_shape=jax.ShapeDtypeStruct((batch, n_heads, seq), _F32),
        compiler_params=_params(("parallel",), _VMEM_MIB["fcumsum"]),
        name="fcumsum",
    )(flog, b_f_pad)


def _attn_kernel(q_ref, k_ref, v_ref, frow_ref, w1_ref, w2_ref, w3_ref,
                 o_ref, w1b_ref, w2b_ref, w3b_ref, v2_scr, *, tq, tk, hps, ahead):
    w1b_ref[...] = w1_ref[...].astype(w1b_ref.dtype)
    w2b_ref[...] = w2_ref[...].astype(w2b_ref.dtype)
    w3b_ref[...] = w3_ref[...].astype(w3b_ref.dtype)

    S = q_ref.shape[1]
    n_blk = S // tq
    hsl = [slice(hh * HEAD_DIM, (hh + 1) * HEAD_DIM) for hh in range(hps)]
    den = slice(HEAD_DIM, 2 * HEAD_DIM)
    tiles = [(i, k0, min(k0 + tk, (i + 1) * tq))
             for i in reversed(range(n_blk)) for k0 in range(0, (i + 1) * tq, tk)]

    for hh in range(hps):
        v2_scr[hh, :, 0:HEAD_DIM] = v_ref[0, :, hsl[hh]]
        v2_scr[hh, :, den] = jnp.ones((S, HEAD_DIM), _BF16)

    rows = lax.broadcasted_iota(jnp.int32, (tq, tq), 0)
    cols = lax.broadcasted_iota(jnp.int32, (tq, tq), 1)
    causal = cols <= rows

    half = tq // 2

    def parts(i, k0, k1):
        if k1 <= i * tq:
            return [(0, tq, k0, k1)]
        return [(0, half, k0, i * tq + half), (half, tq, k0, k1)]

    def scores(t):
        i, k0, k1 = tiles[t]
        for hh in range(hps):
            for r0, r1, c0, c1 in parts(i, k0, k1):
                s = lax.dot_general(q_ref[0, i * tq + r0:i * tq + r1, hsl[hh]],
                                    k_ref[0, c0:c1, hsl[hh]],
                                    (((1,), (1,)), ((), ())), preferred_element_type=_F32)
                pending[(t, hh, r0)] = s - frow_ref[0, 0, hh:hh + 1, c0:c1]

    pending = {}
    for t in range(min(ahead, len(tiles))):
        scores(t)
    fq = m = acc = None
    for t, (i, k0, k1) in enumerate(tiles):
        if t + ahead < len(tiles):
            scores(t + ahead)
        first, last = k0 == 0, k1 == (i + 1) * tq
        if first:
            fq = [jnp.broadcast_to(frow_ref[0, 0, hh:hh + 1, i * tq:(i + 1) * tq],
                                   (LANES, tq)).T[:, 0:1] for hh in range(hps)]
            m = [None] * hps
            acc = [None] * hps
        for hh in range(hps):
            new = []
            for r0, r1, c0, c1 in parts(i, k0, k1):
                a = pending.pop((t, hh, r0))
                if last:
                    d0 = i * tq - c0
                    diag = jnp.where(causal[r0:r1, 0:c1 - i * tq], a[:, d0:], -jnp.inf)
                    a = diag if d0 == 0 else jnp.concatenate([a[:, 0:d0], diag], axis=1)
                fq_r = fq[hh][r0:r1, :]
                m_tile = jnp.max(a, axis=-1, keepdims=True) + fq_r
                m_new = m_tile if first else jnp.maximum(m[hh][r0:r1, :], m_tile)
                p = jnp.exp2(a - (m_new - fq_r)).astype(_BF16)
                pv = jnp.dot(p, v2_scr[hh, c0:c1, :], preferred_element_type=_F32)
                acc_new = pv if first else (
                    jnp.exp2(m[hh][r0:r1, :] - m_new) * acc[hh][r0:r1, :] + pv)
                if last:
                    o_ref[0, i * tq + r0:i * tq + r1, hsl[hh]] = (
                        acc_new[:, 0:HEAD_DIM] / acc_new[:, den]).astype(o_ref.dtype)
                new.append((m_new, acc_new))
            if not last:
                m[hh], acc[hh] = new[0]


def _attention(qkv, frow, weights, *, tq=512, tk=2048, hps=2, ahead=2):
    B, S, W3 = qkv.shape
    W = W3 // 3
    H = W // HEAD_DIM
    n_hp = H // hps
    n_steps = B * n_hp
    heads = lambda b, h: (b, 0, h)
    slab = lambda b, h: (b * n_hp + h, 0)
    w_specs = []
    for w in weights:
        rows = w.shape[0] // n_steps
        assert rows * n_steps == w.shape[0]
        w_specs.append(pl.BlockSpec((rows, w.shape[1]), slab))
    return pl.pallas_call(
        functools.partial(_attn_kernel, tq=tq, tk=tk, hps=hps, ahead=ahead),
        grid=(B, n_hp),
        in_specs=[pl.BlockSpec((1, S, hps * HEAD_DIM), heads),
                  pl.BlockSpec((1, S, hps * HEAD_DIM), lambda b, h: (b, 0, n_hp + h)),
                  pl.BlockSpec((1, S, hps * HEAD_DIM), lambda b, h: (b, 0, 2 * n_hp + h)),
                  pl.BlockSpec((1, 1, hps, S), lambda b, h: (b, h, 0, 0))] + w_specs,
        out_specs=[pl.BlockSpec((1, S, hps * HEAD_DIM), heads)] + w_specs,
        out_shape=[jax.ShapeDtypeStruct((B, S, W), _BF16)]
                  + [jax.ShapeDtypeStruct(w.shape, _BF16) for w in weights],
        scratch_shapes=[pltpu.VMEM((hps, S, 2 * HEAD_DIM), _BF16)],
        compiler_params=_params(("parallel", "parallel"), _VMEM_MIB["attention"]),
        name="attention",
    )(qkv, qkv, qkv, frow.reshape(B, n_hp, hps, S), *weights)


def _outproj_kernel(a_ref, s_ref, w_ref, x_ref, mod_ref, o_ref):
    wa = a_ref.shape[1]
    mix = (jnp.dot(a_ref[...], w_ref[0:wa, :], preferred_element_type=_F32)
           + jnp.dot(s_ref[...], w_ref[wa:, :], preferred_element_type=_F32))
    o_ref[...] = x_ref[...] + mod_ref[0, _GATE1:_GATE1 + 1, :] * mix


def _outproj(attn, sgu, w_out, x2, mod3, *, seq, tm=512):
    M, D = x2.shape
    wa, ws = attn.shape[1], sgu.shape[1]
    tiles_per_seq = seq // tm
    row = lambda m: (m, 0)
    return pl.pallas_call(
        _outproj_kernel,
        grid=(M // tm,),
        in_specs=[pl.BlockSpec((tm, wa), row),
                  pl.BlockSpec((tm, ws), row),
                  pl.BlockSpec((wa + ws, D), lambda m: (0, 0)),
                  pl.BlockSpec((tm, D), row),
                  pl.BlockSpec((1, N_MOD, D), lambda m: (m // tiles_per_seq, 0, 0))],
        out_specs=pl.BlockSpec((tm, D), row),
        out_shape=jax.ShapeDtypeStruct((M, D), _F32),
        compiler_params=_params(("parallel",), _VMEM_MIB["outproj"]),
        name="outproj",
    )(attn, sgu, w_out, x2, mod3)


def _ffn_kernel(x_ref, mod_ref, g_ref, w1_ref, w2_ref, o_ref, h_scr):
    f = pl.program_id(1)

    @pl.when(f == 0)
    def _():
        _norm_modulate(x_ref, g_ref[...], mod_ref[0, _SHIFT2:_SHIFT2 + 1, :],
                       mod_ref[0, _SCALE2:_SCALE2 + 1, :], h_scr, copy_ref=o_ref)

    hid = jnp.dot(h_scr[...], w1_ref[...], preferred_element_type=_F32)
    hid = jnp.square(jnp.maximum(hid, 0.0)).astype(_BF16)
    part = jnp.dot(hid, w2_ref[...], preferred_element_type=_F32)
    o_ref[...] += mod_ref[0, _GATE2:_GATE2 + 1, :] * part


def _ffn(x2, mod3, g_ffn, w1, w2, *, seq, tm=512, tf=1024):
    M, D = x2.shape
    F = w1.shape[1]
    tiles_per_seq = seq // tm
    row = lambda m, f: (m, 0)
    return pl.pallas_call(
        _ffn_kernel,
        grid=(M // tm, F // tf),
        in_specs=[pl.BlockSpec((tm, D), row),
                  pl.BlockSpec((1, N_MOD, D), lambda m, f: (m // tiles_per_seq, 0, 0)),
                  pl.BlockSpec((1, D), lambda m, f: (0, 0)),
                  pl.BlockSpec((D, tf), lambda m, f: (0, f)),
                  pl.BlockSpec((tf, D), lambda m, f: (f, 0))],
        out_specs=pl.BlockSpec((tm, D), row),
        out_shape=jax.ShapeDtypeStruct((M, D), _F32),
        scratch_shapes=[pltpu.VMEM((tm, D), _BF16)],
        compiler_params=_params(("parallel", "arbitrary"), _VMEM_MIB["ffn"]),
        name="ffn",
    )(x2, mod3, g_ffn, w1, w2)


def kernel(x, c, w_ada, b_ada, g_mix, w_in, b_f, g_q, g_k, g_sgu, w_s, b_s, w_out, g_ffn,
           w_ff1, w_ff2):
    B, S, D = x.shape
    depth = w_ada.shape[0]
    H = b_f.shape[1]
    G = w_s.shape[1]
    attn_w = H * HEAD_DIM
    gmlp_w = G * HEAD_DIM
    x2 = x.reshape(B * S, D)
    for l in range(depth):
        o_f = 3 * attn_w
        o_u = o_f + H
        assert attn_w == gmlp_w and w_in.shape[2] == o_u + 2 * gmlp_w
        mod, w6 = _adaln(c, w_ada[l], b_ada[l], w_in[l].T, o_f=o_f, o_u=o_u)
        mod3 = mod.reshape(B, N_MOD, D)
        b_f_pad = jnp.pad(b_f[l], (0, LANES - H)).reshape(1, LANES)

        qkv, sgu, flog = _inproj(
            x2, mod3, g_mix[l].reshape(1, D), w6, g_q[l].reshape(1, HEAD_DIM),
            g_k[l].reshape(1, HEAD_DIM), g_sgu[l].reshape(1, gmlp_w), w_s[l], b_s[l], seq=S)

        frow = _fcumsum(flog, b_f_pad, batch=B, seq=S, n_heads=H)
        attn, w1b, w2b, wob = _attention(
            qkv.reshape(B, S, 3 * attn_w), frow, (w_ff1[l], w_ff2[l], w_out[l]))
        x2 = _outproj(attn.reshape(B * S, attn_w), sgu, wob, x2, mod3, seq=S)
        x2 = _ffn(x2, mod3, g_ffn[l].reshape(1, D), w1b, w2b, seq=S)
    return x2.reshape(B, S, D)
```
